```python
import jax, jax.numpy as jnp
from jax import lax
import numpy as np

D_MODEL = 2048
BATCH = 4
SEQ = 4096
DEPTH = 1

RET_HEADS = 8
RET_QK_DIM = 128
RET_V_DIM = 128
RET_QK_WIDTH = RET_HEADS * RET_QK_DIM
RET_WIDTH = RET_HEADS * RET_V_DIM
SGU_GROUPS = 8
SGU_GROUP_DIM = 128
SGU_WIDTH = SGU_GROUPS * SGU_GROUP_DIM
CHUNK = 128
D_FF = 5632
CONV_WIDTH = 3
ROPE_BASE = 10000.0
EPS = 1e-6
N_BRANCHES = 2
SPLIT_SIZES = (RET_QK_WIDTH, RET_QK_WIDTH, RET_WIDTH, RET_WIDTH, SGU_WIDTH, SGU_WIDTH, D_MODEL, D_MODEL)
IN_COLS = sum(SPLIT_SIZES)
SPLIT_POINTS = tuple(int(v) for v in np.cumsum(SPLIT_SIZES)[:-1])

kernel_name = "hybrid_retention_sgu_gated_block"


def rmsnorm(x, g):
    xf = x.astype(jnp.float32)
    y = xf * lax.rsqrt(jnp.mean(xf * xf, axis=-1, keepdims=True) + EPS)
    return (y * g.astype(jnp.float32)).astype(x.dtype)


def layernorm(x, g, b):
    xf = x.astype(jnp.float32)
    mu = jnp.mean(xf, axis=-1, keepdims=True)
    xc = xf - mu
    y = xc * lax.rsqrt(jnp.mean(xc * xc, axis=-1, keepdims=True) + EPS)
    return (y * g.astype(jnp.float32) + b.astype(jnp.float32)).astype(x.dtype)


def rotary(x, cos, sin):
    x1, x2 = jnp.split(x, 2, axis=-1)
    c = cos[None, :, None, :]
    s = sin[None, :, None, :]
    return jnp.concatenate([x1 * c - x2 * s, x1 * s + x2 * c], axis=-1).astype(x.dtype)


def retention_chunkwise(q, k, v):
    B, S, H, dk = q.shape
    dv = v.shape[-1]
    nc = S // CHUNK
    log_gamma = jnp.log(1.0 - 2.0 ** (-5.0 - jnp.arange(H, dtype=jnp.float32)))
    idx = jnp.arange(CHUNK, dtype=jnp.float32)
    rel = idx[:, None] - idx[None, :]
    causal = rel >= 0
    decay_mask = jnp.where(causal[None], jnp.exp(log_gamma[:, None, None] * jnp.where(causal, rel, 0.0)[None]), 0.0)
    xi = jnp.exp(log_gamma[:, None] * (idx + 1.0)[None])
    zeta = jnp.exp(log_gamma[:, None] * (CHUNK - 1.0 - idx)[None])
    chunk_decay = jnp.exp(log_gamma * CHUNK)
    qc = q.reshape(B, nc, CHUNK, H, dk)
    kc = k.reshape(B, nc, CHUNK, H, dk)
    vc = v.reshape(B, nc, CHUNK, H, dv)
    scores = jnp.einsum('bnihd,bnjhd->bhnij', qc, kc) * decay_mask[None, :, None]
    intra = jnp.einsum('bhnij,bnjhe->bnihe', scores, vc)
    kv = jnp.einsum('bnjhd,hj,bnjhe->nbhde', kc, zeta, vc)

    def step(state, kv_n):
        return state * chunk_decay[None, :, None, None] + kv_n, state

    _, prev_state = lax.scan(step, jnp.zeros_like(kv[0]), kv)
    inter = jnp.einsum('bnihd,nbhde,hi->bnihe', qc, prev_state, xi)
    return (intra + inter).reshape(B, S, H, dv).astype(q.dtype)


def chunked_spatial_gating(u, v, ln_g, ln_b, w_s, b_s):
    B, S, _ = v.shape
    nc = S // CHUNK
    v = layernorm(v, ln_g, ln_b)
    vc = v.reshape(B, nc, CHUNK, SGU_GROUPS, SGU_GROUP_DIM)
    causal = jnp.tril(jnp.ones((CHUNK, CHUNK), dtype=w_s.dtype))
    w = w_s * causal[None]
    mixed = jnp.einsum('gts,bnsgc->bntgc', w, vc) + b_s.T[:, :, None]
    return u * mixed.reshape(B, S, SGU_WIDTH)


def causal_depthwise_conv(h, w, b):
    C = h.shape[-1]
    y = lax.conv_general_dilated(h, w[:, None, :].astype(h.dtype), window_strides=(1,),
                                 padding=[(CONV_WIDTH - 1, 0)],
                                 dimension_numbers=('NWC', 'WIO', 'NWC'),
                                 feature_group_count=C)
    return y + b


def setup_inputs(seed: int = 0) -> dict:
    key = jax.random.key(seed)
    ks = jax.random.split(key, 20)
    f32 = jnp.float32

    def nrm(k, shape, scale):
        return jax.random.normal(k, shape, f32) * scale

    def gain(k, n):
        return 1.0 + 0.02 * jax.random.normal(k, (DEPTH, n), f32)

    return {
        "x": jax.random.normal(ks[0], (BATCH, SEQ, D_MODEL), f32),
        "pre_mix_g": gain(ks[1], D_MODEL),
        "w_in": nrm(ks[2], (DEPTH, D_MODEL, IN_COLS), D_MODEL ** -0.5),
        "w_ret_proj": nrm(ks[3], (DEPTH, RET_WIDTH, D_MODEL), RET_WIDTH ** -0.5),
        "sgu_ln_g": gain(ks[4], SGU_WIDTH),
        "sgu_ln_b": nrm(ks[5], (DEPTH, SGU_WIDTH), 0.02),
        "w_s": nrm(ks[6], (DEPTH, SGU_GROUPS, CHUNK, CHUNK), CHUNK ** -0.5),
        "b_s": 1.0 + nrm(ks[7], (DEPTH, SGU_GROUPS, CHUNK), 0.1),
        "w_sgu_proj": nrm(ks[8], (DEPTH, SGU_WIDTH, D_MODEL), SGU_WIDTH ** -0.5),
        "w_out": nrm(ks[9], (DEPTH, D_MODEL, D_MODEL), D_MODEL ** -0.5),
        "post_mix_g": gain(ks[10], D_MODEL),
        "pre_ffn_g": gain(ks[11], D_MODEL),
        "w_up": nrm(ks[12], (DEPTH, D_MODEL, 2 * D_FF), D_MODEL ** -0.5),
        "conv_w": nrm(ks[13], (DEPTH, CONV_WIDTH, 2 * D_FF), CONV_WIDTH ** -0.5),
        "conv_b": nrm(ks[14], (DEPTH, 2 * D_FF), 0.02),
        "w_down": nrm(ks[15], (DEPTH, D_FF, D_MODEL), D_FF ** -0.5),
        "post_ffn_g": gain(ks[16], D_MODEL),
    }


def reference(x, pre_mix_g, w_in, w_ret_proj, sgu_ln_g, sgu_ln_b, w_s, b_s, w_sgu_proj, w_out,
              post_mix_g, pre_ffn_g, w_up, conv_w, conv_b, w_down, post_ffn_g):
    B, S, _ = x.shape
    inv_freq = 1.0 / (ROPE_BASE ** (jnp.arange(0, RET_QK_DIM, 2, dtype=jnp.float32) / RET_QK_DIM))
    ang = jnp.arange(S, dtype=jnp.float32)[:, None] * inv_freq[None, :]
    cos, sin = jnp.cos(ang), jnp.sin(ang)

    for l in range(DEPTH):
        h = rmsnorm(x, pre_mix_g[l])
        proj = h @ w_in[l]
        q, k, v, rg, su, sv, g_ret, g_sgu = jnp.split(proj, SPLIT_POINTS, axis=-1)

        q = rotary(q.reshape(B, S, RET_HEADS, RET_QK_DIM), cos, sin)
        k = rotary(k.reshape(B, S, RET_HEADS, RET_QK_DIM), cos, sin) * (RET_QK_DIM ** -0.5)
        v = v.reshape(B, S, RET_HEADS, RET_V_DIM)
        ret = retention_chunkwise(q, k, v)
        ret = rmsnorm(ret, jnp.ones((RET_V_DIM,), jnp.float32)).reshape(B, S, RET_WIDTH)
        ret = jax.nn.silu(rg) * ret
        a = ret @ w_ret_proj[l]

        sgu = chunked_spatial_gating(jax.nn.gelu(su), jax.nn.gelu(sv), sgu_ln_g[l], sgu_ln_b[l], w_s[l], b_s[l])
        bb = sgu @ w_sgu_proj[l]

        merged = jax.nn.sigmoid(g_ret) * a + jax.nn.sigmoid(g_sgu) * bb
        x = x + rmsnorm(merged @ w_out[l], post_mix_g[l])

        h = rmsnorm(x, pre_ffn_g[l])
        up = causal_depthwise_conv(h @ w_up[l], conv_w[l], conv_b[l])
        gate, val = jnp.split(up, 2, axis=-1)
        y = (jax.nn.gelu(gate, approximate=True) * val) @ w_down[l]
        x = x + rmsnorm(y, post_ffn_g[l])
    return x
```

```python
import functools

import jax
import jax.numpy as jnp
import numpy as np
from jax import lax
from jax.experimental import pallas as pl
from jax.experimental.pallas import tpu as pltpu

D_MODEL = 2048
RET_HEADS = 8
HEAD_DIM = 128
SGU_GROUPS = 8
SEG = 1024
CHUNK = 128
D_FF = 5632
CONV_WIDTH = 3
ROPE_BASE = 10000.0
EPS = 1e-6
IN_COLS = 6 * SEG + 2 * D_MODEL

F32 = jnp.float32
BF16 = jnp.bfloat16

VMEM_LIMIT_BYTES = 56 * 1024 * 1024

IN_PROJ_ROWS = 512
MIXER_ROWS = 256
FFN_ROWS = 512
FFN_COLS = 512
HALO = 8


def _rms(x, g):
    ms = jnp.mean(x * x, axis=-1, keepdims=True)
    return x * lax.rsqrt(ms + EPS) * g


def _in_proj_kernel(x_ref, g_ref, w_ref, cos_ref, sin_ref, lng_ref, lnb_ref, o_ref, h_ref):
    j = pl.program_id(1)

    @pl.when(j == 0)
    def _():
        h_ref[...] = _rms(x_ref[...], g_ref[...]).astype(BF16)

    acc = jnp.dot(h_ref[...], w_ref[...], preferred_element_type=F32)

    def rotary_store(scale):
        c = cos_ref[...]
        s = sin_ref[...]
        for h in range(RET_HEADS):
            cols = slice(h * HEAD_DIM, (h + 1) * HEAD_DIM)
            xh = acc[:, cols]
            r = xh * c + pltpu.roll(xh, HEAD_DIM // 2, 1) * s
            if scale is not None:
                r = r * scale
            o_ref[:, cols] = r.astype(BF16)

    @pl.when(j == 0)
    def _():
        rotary_store(None)

    @pl.when(j == 1)
    def _():
        rotary_store(HEAD_DIM ** -0.5)

    @pl.when(j == 2)
    def _():
        o_ref[...] = acc.astype(BF16)

    @pl.when(j == 3)
    def _():
        o_ref[...] = jax.nn.silu(acc).astype(BF16)

    @pl.when(j == 4)
    def _():
        o_ref[...] = jax.nn.gelu(acc).astype(BF16)

    @pl.when(j == 5)
    def _():
        v = jax.nn.gelu(acc)
        mu = jnp.mean(v, axis=-1, keepdims=True)
        vc = v - mu
        y = vc * lax.rsqrt(jnp.mean(vc * vc, axis=-1, keepdims=True) + EPS)
        o_ref[...] = (y * lng_ref[...] + lnb_ref[...]).astype(BF16)

    @pl.when(j >= 6)
    def _():
        o_ref[...] = jax.nn.sigmoid(acc).astype(BF16)


def _in_proj(x2, pre_g, w_in, cosf, sinf, ln_g, ln_b, seq):
    t = x2.shape[0]
    tm = IN_PROJ_ROWS
    pos_blocks = seq // tm
    return pl.pallas_call(
        _in_proj_kernel,
        grid=(t // tm, IN_COLS // SEG),
        in_specs=[
            pl.BlockSpec((tm, D_MODEL), lambda i, j: (i, 0)),
            pl.BlockSpec((1, D_MODEL), lambda i, j: (0, 0)),
            pl.BlockSpec((D_MODEL, SEG), lambda i, j: (0, j)),
            pl.BlockSpec((tm, HEAD_DIM), lambda i, j: (i % pos_blocks, 0)),
            pl.BlockSpec((tm, HEAD_DIM), lambda i, j: (i % pos_blocks, 0)),
            pl.BlockSpec((1, SEG), lambda i, j: (0, 0)),
            pl.BlockSpec((1, SEG), lambda i, j: (0, 0)),
        ],
        out_specs=pl.BlockSpec((tm, SEG), lambda i, j: (i, j)),
        out_shape=jax.ShapeDtypeStruct((t, IN_COLS), BF16),
        scratch_shapes=[pltpu.VMEM((tm, D_MODEL), BF16)],
        compiler_params=pltpu.CompilerParams(
            dimension_semantics=("arbitrary", "arbitrary"),
            vmem_limit_bytes=VMEM_LIMIT_BYTES),
        name="in_proj",
    )(x2, pre_g, w_in, cosf, sinf, ln_g, ln_b)


def _mixer_kernel(cdec_ref, q_ref, k_ref, v_ref, rg_ref, su_ref, sv_ref, gr_ref, gs_ref, x_ref,
                  wret_ref, wsgu_ref, wout_ref, ws_ref, bs_ref, pmg_ref,
                  dmask_ref, xi_ref, zeta_ref,
                  o_ref, state_ref, ret_buf, sgu_buf):
    @pl.when(pl.program_id(1) == 0)
    def _():
        state_ref[...] = jnp.zeros_like(state_ref)

    row_id = lax.broadcasted_iota(jnp.int32, (CHUNK, CHUNK), 0)
    col_id = lax.broadcasted_iota(jnp.int32, (CHUNK, CHUNK), 1)
    causal = row_id >= col_id
    w_mix = [jnp.where(causal, ws_ref[g], 0.0).astype(BF16) for g in range(SGU_GROUPS)]

    for c in range(MIXER_ROWS // CHUNK):
        rows = slice(c * CHUNK, (c + 1) * CHUNK)
        for h in range(RET_HEADS):
            cols = slice(h * HEAD_DIM, (h + 1) * HEAD_DIM)
            qh = q_ref[rows, cols]
            kh = k_ref[rows, cols]
            vh = v_ref[rows, cols]
            scores = lax.dot_general(qh, kh, (((1,), (1,)), ((), ())), preferred_element_type=F32)
            scores = scores * dmask_ref[h]
            intra = jnp.dot(scores.astype(BF16), vh, preferred_element_type=F32)
            state = state_ref[h]
            inter = jnp.dot(qh, state.astype(BF16), preferred_element_type=F32) * xi_ref[h]
            kz = (kh.astype(F32) * zeta_ref[h]).astype(BF16)
            kv = lax.dot_general(kz, vh, (((0,), (0,)), ((), ())), preferred_element_type=F32)
            state_ref[h] = state * cdec_ref[h] + kv
            o = intra + inter
            o = o * lax.rsqrt(jnp.mean(o * o, axis=-1, keepdims=True) + EPS)
            ret_buf[rows, cols] = (rg_ref[rows, cols].astype(F32) * o).astype(BF16)
        for g in range(SGU_GROUPS):
            cols = slice(g * HEAD_DIM, (g + 1) * HEAD_DIM)
            mixed = jnp.dot(w_mix[g], sv_ref[rows, cols], preferred_element_type=F32) + bs_ref[g]
            sgu_buf[rows, cols] = (su_ref[rows, cols].astype(F32) * mixed).astype(BF16)

    a = jnp.dot(ret_buf[...], wret_ref[...], preferred_element_type=F32)
    b = jnp.dot(sgu_buf[...], wsgu_ref[...], preferred_element_type=F32)
    merged = gr_ref[...].astype(F32) * a + gs_ref[...].astype(F32) * b
    z = jnp.dot(merged.astype(BF16), wout_ref[...], preferred_element_type=F32)
    o_ref[...] = x_ref[...] + _rms(z, pmg_ref[...])


def _mixer(proj, x2, w_ret, w_sgu, w_out, w_s, bs_b, post_g, dmask, xi_b, zeta_b, cdec, batch, seq):
    t = x2.shape[0]
    tq = MIXER_ROWS
    nq = seq // tq

    def tok(width, col):
        return pl.BlockSpec((tq, width), lambda b, n, *_: (b * nq + n, col))

    def resident(shape):
        zeros = (0,) * len(shape)
        return pl.BlockSpec(shape, lambda b, n, *_: zeros, pipeline_mode=pl.Buffered(1))

    grid_spec = pltpu.PrefetchScalarGridSpec(
        num_scalar_prefetch=1,
        grid=(batch, nq),
        in_specs=[
            tok(SEG, 0), tok(SEG, 1), tok(SEG, 2), tok(SEG, 3), tok(SEG, 4), tok(SEG, 5),
            tok(D_MODEL, 3), tok(D_MODEL, 4),
            tok(D_MODEL, 0),
            resident((SEG, D_MODEL)), resident((SEG, D_MODEL)), resident((D_MODEL, D_MODEL)),
            resident((SGU_GROUPS, CHUNK, CHUNK)), resident((SGU_GROUPS, CHUNK, CHUNK)),
            resident((1, D_MODEL)),
            resident((RET_HEADS, CHUNK, CHUNK)), resident((RET_HEADS, CHUNK, CHUNK)),
            resident((RET_HEADS, CHUNK, CHUNK)),
        ],
        out_specs=pl.BlockSpec((tq, D_MODEL), lambda b, n, *_: (b * nq + n, 0)),
        scratch_shapes=[
            pltpu.VMEM((RET_HEADS, HEAD_DIM, HEAD_DIM), F32),
            pltpu.VMEM((tq, SEG), BF16),
            pltpu.VMEM((tq, SEG), BF16),
        ],
    )
    return pl.pallas_call(
        _mixer_kernel,
        grid_spec=grid_spec,
        out_shape=jax.ShapeDtypeStruct((t, D_MODEL), F32),
        compiler_params=pltpu.CompilerParams(
            dimension_semantics=("arbitrary", "arbitrary"),
            vmem_limit_bytes=VMEM_LIMIT_BYTES),
        name="mixer",
    )(cdec, proj, proj, proj, proj, proj, proj, proj, proj, x2,
      w_ret, w_sgu, w_out, w_s, bs_b, post_g, dmask, xi_b, zeta_b)


def _ffn_kernel(x_ref, g_ref, wg_ref, wv_ref, cwg_ref, cwv_ref, cbg_ref, cbv_ref, wd_ref, pg_ref,
                o_ref, h_ref, acc_ref, tail_g_ref, tail_v_ref, *, tiles_per_seq):
    i = pl.program_id(0)
    j = pl.program_id(1)

    @pl.when(j == 0)
    def _():
        h_ref[...] = _rms(x_ref[...], g_ref[...]).astype(BF16)
        acc_ref[...] = jnp.zeros_like(acc_ref)

    h = h_ref[...]
    rows = h.shape[0]
    row_id = lax.broadcasted_iota(jnp.int32, (rows, FFN_COLS), 0)
    first_tile = (i % tiles_per_seq) == 0

    @pl.when(first_tile)
    def _():
        tail_g_ref[j] = jnp.zeros((HALO, FFN_COLS), F32)
        tail_v_ref[j] = jnp.zeros((HALO, FFN_COLS), F32)

    def conv(w_ref, cw_ref, cb_ref, tail_ref):
        u = jnp.dot(h, w_ref[...], preferred_element_type=F32)
        prev = tail_ref[j]
        tail_ref[j] = u[rows - HALO:, :]
        u1 = pltpu.roll(u, 1, 0)
        u2 = pltpu.roll(u, 2, 0)
        p1 = prev[HALO - 1:HALO, :]
        p2 = prev[HALO - 2:HALO - 1, :]
        u1 = jnp.where(row_id == 0, p1, u1)
        u2 = jnp.where(row_id == 0, p2, jnp.where(row_id == 1, p1, u2))
        cw = cw_ref[...]
        return u2 * cw[0:1, :] + u1 * cw[1:2, :] + u * cw[2:3, :] + cb_ref[...]

    gate = conv(wg_ref, cwg_ref, cbg_ref, tail_g_ref)
    val = conv(wv_ref, cwv_ref, cbv_ref, tail_v_ref)
    act = (jax.nn.gelu(gate, approximate=True) * val).astype(BF16)
    acc_ref[...] += jnp.dot(act, wd_ref[...], preferred_element_type=F32)

    @pl.when(j == pl.num_programs(1) - 1)
    def _():
        o_ref[...] = x_ref[...] + _rms(acc_ref[...], pg_ref[...])


def _ffn(x1, pre_g, w_up, conv_w, conv_b, w_down, post_g, seq):
    t = x1.shape[0]
    tm, tf = FFN_ROWS, FFN_COLS
    nf = D_FF // tf
    kernel = functools.partial(_ffn_kernel, tiles_per_seq=seq // tm)
    return pl.pallas_call(
        kernel,
        grid=(t // tm, nf),
        in_specs=[
            pl.BlockSpec((tm, D_MODEL), lambda i, j: (i, 0)),
            pl.BlockSpec((1, D_MODEL), lambda i, j: (0, 0)),
            pl.BlockSpec((D_MODEL, tf), lambda i, j: (0, j)),
            pl.BlockSpec((D_MODEL, tf), lambda i, j: (0, j + nf)),
            pl.BlockSpec((CONV_WIDTH, tf), lambda i, j: (0, j)),
            pl.BlockSpec((CONV_WIDTH, tf), lambda i, j: (0, j + nf)),
            pl.BlockSpec((1, tf), lambda i, j: (0, j)),
            pl.BlockSpec((1, tf), lambda i, j: (0, j + nf)),
            pl.BlockSpec((tf, D_MODEL), lambda i, j: (j, 0)),
            pl.BlockSpec((1, D_MODEL), lambda i, j: (0, 0)),
        ],
        out_specs=pl.BlockSpec((tm, D_MODEL), lambda i, j: (i, 0)),
        out_shape=jax.ShapeDtypeStruct((t, D_MODEL), F32),
        scratch_shapes=[
            pltpu.VMEM((tm, D_MODEL), BF16),
            pltpu.VMEM((tm, D_MODEL), F32),
            pltpu.VMEM((nf, HALO, tf), F32),
            pltpu.VMEM((nf, HALO, tf), F32),
        ],
        compiler_params=pltpu.CompilerParams(
            dimension_semantics=("arbitrary", "arbitrary"),
            vmem_limit_bytes=VMEM_LIMIT_BYTES),
        name="ffn",
    )(x1, pre_g, w_up, w_up, conv_w, conv_w, conv_b, conv_b, w_down, post_g)


def _rotary_tables(seq):
    inv_freq = 1.0 / (ROPE_BASE ** (jnp.arange(0, HEAD_DIM, 2, dtype=F32) / HEAD_DIM))
    ang = jnp.arange(seq, dtype=F32)[:, None] * inv_freq[None, :]
    cos, sin = jnp.cos(ang), jnp.sin(ang)
    return jnp.concatenate([cos, cos], axis=-1), jnp.concatenate([-sin, sin], axis=-1)


def _decay_tables():
    log_gamma = jnp.log(1.0 - 2.0 ** (-5.0 - jnp.arange(RET_HEADS, dtype=F32)))
    idx = jnp.arange(CHUNK, dtype=F32)
    rel = idx[:, None] - idx[None, :]
    causal = rel >= 0
    dmask = jnp.where(causal[None],
                      jnp.exp(log_gamma[:, None, None] * jnp.where(causal, rel, 0.0)[None]), 0.0)
    xi = jnp.exp(log_gamma[:, None] * (idx + 1.0)[None])
    zeta = jnp.exp(log_gamma[:, None] * (CHUNK - 1.0 - idx)[None])
    cdec = jnp.exp(log_gamma * CHUNK)
    bshape = (RET_HEADS, CHUNK, CHUNK)
    return (dmask, jnp.broadcast_to(xi[:, :, None], bshape),
            jnp.broadcast_to(zeta[:, :, None], bshape), cdec)


def kernel(x, pre_mix_g, w_in, w_ret_proj, sgu_ln_g, sgu_ln_b, w_s, b_s, w_sgu_proj, w_out,
           post_mix_g, pre_ffn_g, w_up, conv_w, conv_b, w_down, post_ffn_g):
    batch, seq, _ = x.shape
    depth = w_in.shape[0]
    cosf, sinf = _rotary_tables(seq)
    dmask, xi_b, zeta_b, cdec = _decay_tables()
    x2 = x.reshape(batch * seq, D_MODEL)
    for l in range(depth):
        proj = _in_proj(x2, pre_mix_g[l][None], w_in[l].astype(BF16), cosf, sinf,
                        sgu_ln_g[l][None], sgu_ln_b[l][None], seq)
        bs_b = jnp.broadcast_to(b_s[l][:, :, None], (SGU_GROUPS, CHUNK, CHUNK))
        x2 = _mixer(proj, x2, w_ret_proj[l].astype(BF16), w_sgu_proj[l].astype(BF16),
                    w_out[l].astype(BF16), w_s[l], bs_b, post_mix_g[l][None],
                    dmask, xi_b, zeta_b, cdec, batch, seq)
        x2 = _ffn(x2, pre_ffn_g[l][None], w_up[l].astype(BF16), conv_w[l], conv_b[l][None],
                  w_down[l].astype(BF16), post_ffn_g[l][None], seq)
    return x2.reshape(batch, seq, D_MODEL)
```

```python
import functools

import jax
import jax.numpy as jnp
import numpy as np
from jax import lax
from jax.experimental import pallas as pl
from jax.experimental.pallas import tpu as pltpu

D_MODEL = 2048
RET_HEADS = 8
HEAD_DIM = 128
SGU_GROUPS = 8
SEG = 1024
CHUNK = 128
D_FF = 5632
CONV_WIDTH = 3
ROPE_BASE = 10000.0
EPS = 1e-6
IN_COLS = 6 * SEG + 2 * D_MODEL

F32 = jnp.float32
BF16 = jnp.bfloat16

VMEM_LIMIT_BYTES = 56 * 1024 * 1024

IN_PROJ_ROWS = 512
MIXER_ROWS = 256
FFN_ROWS = 512
FFN_COLS = 512
HALO = 8


def _rms(x, g):
    ms = jnp.mean(x * x, axis=-1, keepdims=True)
    return x * lax.rsqrt(ms + EPS) * g


def _in_proj_kernel(x_ref, g_ref, w_ref, cos_ref, sin_ref, lng_ref, lnb_ref, o_ref, h_ref):
    j = pl.program_id(1)

    @pl.when(j == 0)
    def _():
        h_ref[...] = _rms(x_ref[...], g_ref[...]).astype(BF16)

    def proj():
        return jnp.dot(h_ref[...], w_ref[...], preferred_element_type=F32)

    def sigmoid(v):
        return 0.5 * jnp.tanh(0.5 * v) + 0.5

    def rotary_store(scale):
        acc = proj()
        c = cos_ref[...]
        s = sin_ref[...]
        for h in range(RET_HEADS):
            cols = slice(h * HEAD_DIM, (h + 1) * HEAD_DIM)
            xh = acc[:, cols]
            r = xh * c + pltpu.roll(xh, HEAD_DIM // 2, 1) * s
            if scale is not None:
                r = r * scale
            o_ref[:, cols] = r.astype(BF16)

    @pl.when(j == 0)
    def _():
        rotary_store(None)

    @pl.when(j == 1)
    def _():
        rotary_store(HEAD_DIM ** -0.5)

    @pl.when(j == 2)
    def _():
        o_ref[...] = proj().astype(BF16)

    @pl.when(j == 3)
    def _():
        acc = proj()
        o_ref[...] = (acc * sigmoid(acc)).astype(BF16)

    @pl.when(j == 4)
    def _():
        o_ref[...] = jax.nn.gelu(proj()).astype(BF16)

    @pl.when(j == 5)
    def _():
        v = jax.nn.gelu(proj())
        mu = jnp.mean(v, axis=-1, keepdims=True)
        vc = v - mu
        y = vc * lax.rsqrt(jnp.mean(vc * vc, axis=-1, keepdims=True) + EPS)
        o_ref[...] = (y * lng_ref[...] + lnb_ref[...]).astype(BF16)

    @pl.when(j >= 6)
    def _():
        o_ref[...] = sigmoid(proj()).astype(BF16)


def _in_proj(x2, pre_g, w_in, cosf, sinf, ln_g, ln_b, seq):
    t = x2.shape[0]
    tm = IN_PROJ_ROWS
    pos_blocks = seq // tm
    return pl.pallas_call(
        _in_proj_kernel,
        grid=(t // tm, IN_COLS // SEG),
        in_specs=[
            pl.BlockSpec((tm, D_MODEL), lambda i, j: (i, 0)),
            pl.BlockSpec((1, D_MODEL), lambda i, j: (0, 0)),
            pl.BlockSpec((D_MODEL, SEG), lambda i, j: (0, j)),
            pl.BlockSpec((tm, HEAD_DIM), lambda i, j: (i % pos_blocks, 0)),
            pl.BlockSpec((tm, HEAD_DIM), lambda i, j: (i % pos_blocks, 0)),
            pl.BlockSpec((1, SEG), lambda i, j: (0, 0)),
            pl.BlockSpec((1, SEG), lambda i, j: (0, 0)),
        ],
        out_specs=pl.BlockSpec((tm, SEG), lambda i, j: (i, j)),
        out_shape=jax.ShapeDtypeStruct((t, IN_COLS), BF16),
        scratch_shapes=[pltpu.VMEM((tm, D_MODEL), BF16)],
        compiler_params=pltpu.CompilerParams(
            dimension_semantics=("arbitrary", "arbitrary"),
            vmem_limit_bytes=VMEM_LIMIT_BYTES),
        name="in_proj",
    )(x2, pre_g, w_in, cosf, sinf, ln_g, ln_b)


def _mixer_kernel(cdec_ref, q_ref, k_ref, v_ref, rg_ref, su_ref, sv_ref, gr_ref, gs_ref, x_ref,
                  wret_ref, wsgu_ref, wout_ref, ws_ref, bs_ref, pmg_ref,
                  dmask_ref, xi_ref, zeta_ref,
                  o_ref, state_ref, ret_buf, sgu_buf):
    @pl.when(pl.program_id(1) == 0)
    def _():
        state_ref[...] = jnp.zeros_like(state_ref)

    row_id = lax.broadcasted_iota(jnp.int32, (CHUNK, CHUNK), 0)
    col_id = lax.broadcasted_iota(jnp.int32, (CHUNK, CHUNK), 1)
    causal = row_id >= col_id
    w_mix = [jnp.where(causal, ws_ref[g], 0.0).astype(BF16) for g in range(SGU_GROUPS)]

    for c in range(MIXER_ROWS // CHUNK):
        rows = slice(c * CHUNK, (c + 1) * CHUNK)
        for h in range(RET_HEADS):
            cols = slice(h * HEAD_DIM, (h + 1) * HEAD_DIM)
            qh = q_ref[rows, cols]
            kh = k_ref[rows, cols]
            vh = v_ref[rows, cols]
            scores = lax.dot_general(qh, kh, (((1,), (1,)), ((), ())), preferred_element_type=F32)
            scores = scores * dmask_ref[h]
            intra = jnp.dot(scores.astype(BF16), vh, preferred_element_type=F32)
            state = state_ref[h]
            inter = jnp.dot(qh, state.astype(BF16), preferred_element_type=F32) * xi_ref[h]
            kz = (kh.astype(F32) * zeta_ref[h]).astype(BF16)
            kv = lax.dot_general(kz, vh, (((0,), (0,)), ((), ())), preferred_element_type=F32)
            state_ref[h] = state * cdec_ref[h] + kv
            o = intra + inter
            o = o * lax.rsqrt(jnp.mean(o * o, axis=-1, keepdims=True) + EPS)
            ret_buf[rows, cols] = (rg_ref[rows, cols].astype(F32) * o).astype(BF16)
        for g in range(SGU_GROUPS):
            cols = slice(g * HEAD_DIM, (g + 1) * HEAD_DIM)
            mixed = jnp.dot(w_mix[g], sv_ref[rows, cols], preferred_element_type=F32) + bs_ref[g]
            sgu_buf[rows, cols] = (su_ref[rows, cols].astype(F32) * mixed).astype(BF16)

    a = jnp.dot(ret_buf[...], wret_ref[...], preferred_element_type=F32)
    b = jnp.dot(sgu_buf[...], wsgu_ref[...], preferred_element_type=F32)
    merged = gr_ref[...].astype(F32) * a + gs_ref[...].astype(F32) * b
    z = jnp.dot(merged.astype(BF16), wout_ref[...], preferred_element_type=F32)
    o_ref[...] = x_ref[...] + _rms(z, pmg_ref[...])


def _mixer(proj, x2, w_ret, w_sgu, w_out, w_s, bs_b, post_g, dmask, xi_b, zeta_b, cdec, batch, seq):
    t = x2.shape[0]
    tq = MIXER_ROWS
    nq = seq // tq

    def tok(width, col):
        return pl.BlockSpec((tq, width), lambda b, n, *_: (b * nq + n, col))

    def resident(shape):
        zeros = (0,) * len(shape)
        return pl.BlockSpec(shape, lambda b, n, *_: zeros, pipeline_mode=pl.Buffered(1))

    grid_spec = pltpu.PrefetchScalarGridSpec(
        num_scalar_prefetch=1,
        grid=(batch, nq),
        in_specs=[
            tok(SEG, 0), tok(SEG, 1), tok(SEG, 2), tok(SEG, 3), tok(SEG, 4), tok(SEG, 5),
            tok(D_MODEL, 3), tok(D_MODEL, 4),
            tok(D_MODEL, 0),
            resident((SEG, D_MODEL)), resident((SEG, D_MODEL)), resident((D_MODEL, D_MODEL)),
            resident((SGU_GROUPS, CHUNK, CHUNK)), resident((SGU_GROUPS, CHUNK, CHUNK)),
            resident((1, D_MODEL)),
            resident((RET_HEADS, CHUNK, CHUNK)), resident((RET_HEADS, CHUNK, CHUNK)),
            resident((RET_HEADS, CHUNK, CHUNK)),
        ],
        out_specs=pl.BlockSpec((tq, D_MODEL), lambda b, n, *_: (b * nq + n, 0)),
        scratch_shapes=[
            pltpu.VMEM((RET_HEADS, HEAD_DIM, HEAD_DIM), F32),
            pltpu.VMEM((tq, SEG), BF16),
            pltpu.VMEM((tq, SEG), BF16),
        ],
    )
    return pl.pallas_call(
        _mixer_kernel,
        grid_spec=grid_spec,
        out_shape=jax.ShapeDtypeStruct((t, D_MODEL), F32),
        compiler_params=pltpu.CompilerParams(
            dimension_semantics=("arbitrary", "arbitrary"),
            vmem_limit_bytes=VMEM_LIMIT_BYTES),
        name="mixer",
    )(cdec, proj, proj, proj, proj, proj, proj, proj, proj, x2,
      w_ret, w_sgu, w_out, w_s, bs_b, post_g, dmask, xi_b, zeta_b)


def _ffn_kernel(x_ref, g_ref, wg_ref, wv_ref, cwg_ref, cwv_ref, cbg_ref, cbv_ref, wd_ref, pg_ref,
                o_ref, h_ref, acc_ref, tail_g_ref, tail_v_ref, *, tiles_per_seq):
    i = pl.program_id(0)
    j = pl.program_id(1)

    @pl.when(j == 0)
    def _():
        h_ref[...] = _rms(x_ref[...], g_ref[...]).astype(BF16)
        acc_ref[...] = jnp.zeros_like(acc_ref)

    h = h_ref[...]
    rows = h.shape[0]
    row_id = lax.broadcasted_iota(jnp.int32, (rows, FFN_COLS), 0)
    first_tile = (i % tiles_per_seq) == 0

    @pl.when(first_tile)
    def _():
        tail_g_ref[j] = jnp.zeros((HALO, FFN_COLS), F32)
        tail_v_ref[j] = jnp.zeros((HALO, FFN_COLS), F32)

    def conv(w_ref, cw_ref, cb_ref, tail_ref):
        u = jnp.dot(h, w_ref[...], preferred_element_type=F32)
        prev = tail_ref[j]
        tail_ref[j] = u[rows - HALO:, :]
        u1 = pltpu.roll(u, 1, 0)
        u2 = pltpu.roll(u, 2, 0)
        p1 = prev[HALO - 1:HALO, :]
        p2 = prev[HALO - 2:HALO - 1, :]
        u1 = jnp.where(row_id == 0, p1, u1)
        u2 = jnp.where(row_id == 0, p2, jnp.where(row_id == 1, p1, u2))
        cw = cw_ref[...]
        return u2 * cw[0:1, :] + u1 * cw[1:2, :] + u * cw[2:3, :] + cb_ref[...]

    gate = conv(wg_ref, cwg_ref, cbg_ref, tail_g_ref)
    val = conv(wv_ref, cwv_ref, cbv_ref, tail_v_ref)
    act = (jax.nn.gelu(gate, approximate=True) * val).astype(BF16)
    acc_ref[...] += jnp.dot(act, wd_ref[...], preferred_element_type=F32)

    @pl.when(j == pl.num_programs(1) - 1)
    def _():
        o_ref[...] = x_ref[...] + _rms(acc_ref[...], pg_ref[...])


def _ffn(x1, pre_g, w_up, conv_w, conv_b, w_down, post_g, seq):
    t = x1.shape[0]
    tm, tf = FFN_ROWS, FFN_COLS
    nf = D_FF // tf
    kernel = functools.partial(_ffn_kernel, tiles_per_seq=seq // tm)
    return pl.pallas_call(
        kernel,
        grid=(t // tm, nf),
        in_specs=[
            pl.BlockSpec((tm, D_MODEL), lambda i, j: (i, 0)),
            pl.BlockSpec((1, D_MODEL), lambda i, j: (0, 0)),
            pl.BlockSpec((D_MODEL, tf), lambda i, j: (0, j)),
            pl.BlockSpec((D_MODEL, tf), lambda i, j: (0, j + nf)),
            pl.BlockSpec((CONV_WIDTH, tf), lambda i, j: (0, j)),
            pl.BlockSpec((CONV_WIDTH, tf), lambda i, j: (0, j + nf)),
            pl.BlockSpec((1, tf), lambda i, j: (0, j)),
            pl.BlockSpec((1, tf), lambda i, j: (0, j + nf)),
            pl.BlockSpec((tf, D_MODEL), lambda i, j: (j, 0)),
            pl.BlockSpec((1, D_MODEL), lambda i, j: (0, 0)),
        ],
        out_specs=pl.BlockSpec((tm, D_MODEL), lambda i, j: (i, 0)),
        out_shape=jax.ShapeDtypeStruct((t, D_MODEL), F32),
        scratch_shapes=[
            pltpu.VMEM((tm, D_MODEL), BF16),
            pltpu.VMEM((tm, D_MODEL), F32),
            pltpu.VMEM((nf, HALO, tf), F32),
            pltpu.VMEM((nf, HALO, tf), F32),
        ],
        compiler_params=pltpu.CompilerParams(
            dimension_semantics=("arbitrary", "arbitrary"),
            vmem_limit_bytes=VMEM_LIMIT_BYTES),
        name="ffn",
    )(x1, pre_g, w_up, w_up, conv_w, conv_w, conv_b, conv_b, w_down, post_g)


def _rotary_tables(seq):
    inv_freq = 1.0 / (ROPE_BASE ** (jnp.arange(0, HEAD_DIM, 2, dtype=F32) / HEAD_DIM))
    ang = jnp.arange(seq, dtype=F32)[:, None] * inv_freq[None, :]
    cos, sin = jnp.cos(ang), jnp.sin(ang)
    return jnp.concatenate([cos, cos], axis=-1), jnp.concatenate([-sin, sin], axis=-1)


def _decay_tables():
    log_gamma = jnp.log(1.0 - 2.0 ** (-5.0 - jnp.arange(RET_HEADS, dtype=F32)))
    idx = jnp.arange(CHUNK, dtype=F32)
    rel = idx[:, None] - idx[None, :]
    causal = rel >= 0
    dmask = jnp.where(causal[None],
                      jnp.exp(log_gamma[:, None, None] * jnp.where(causal, rel, 0.0)[None]), 0.0)
    xi = jnp.exp(log_gamma[:, None] * (idx + 1.0)[None])
    zeta = jnp.exp(log_gamma[:, None] * (CHUNK - 1.0 - idx)[None])
    cdec = jnp.exp(log_gamma * CHUNK)
    bshape = (RET_HEADS, CHUNK, CHUNK)
    return (dmask, jnp.broadcast_to(xi[:, :, None], bshape),
            jnp.broadcast_to(zeta[:, :, None], bshape), cdec)


def kernel(x, pre_mix_g, w_in, w_ret_proj, sgu_ln_g, sgu_ln_b, w_s, b_s, w_sgu_proj, w_out,
           post_mix_g, pre_ffn_g, w_up, conv_w, conv_b, w_down, post_ffn_g):
    batch, seq, _ = x.shape
    depth = w_in.shape[0]
    cosf, sinf = _rotary_tables(seq)
    dmask, xi_b, zeta_b, cdec = _decay_tables()
    x2 = x.reshape(batch * seq, D_MODEL)
    for l in range(depth):
        proj = _in_proj(x2, pre_mix_g[l][None], w_in[l].astype(BF16), cosf, sinf,
                        sgu_ln_g[l][None], sgu_ln_b[l][None], seq)
        bs_b = jnp.broadcast_to(b_s[l][:, :, None], (SGU_GROUPS, CHUNK, CHUNK))
        x2 = _mixer(proj, x2, w_ret_proj[l].astype(BF16), w_sgu_proj[l].astype(BF16),
                    w_out[l].astype(BF16), w_s[l], bs_b, post_mix_g[l][None],
                    dmask, xi_b, zeta_b, cdec, batch, seq)
        x2 = _ffn(x2, pre_ffn_g[l][None], w_up[l].astype(BF16), conv_w[l], conv_b[l][None],
                  w_down[l].astype(BF16), post_ffn_g[l][None], seq)
    return x2.reshape(batch, seq, D_MODEL)
```

```python
import functools

import jax
import jax.numpy as jnp
import numpy as np
from jax import lax
from jax.experimental import pallas as pl
from jax.experimental.pallas import tpu as pltpu

D_MODEL = 2048
RET_HEADS = 8
HEAD_DIM = 128
SGU_GROUPS = 8
SEG = 1024
CHUNK = 128
D_FF = 5632
CONV_WIDTH = 3
ROPE_BASE = 10000.0
EPS = 1e-6
GELU_C = float(np.sqrt(2.0 / np.pi))
IN_COLS = 6 * SEG + 2 * D_MODEL

F32 = jnp.float32
BF16 = jnp.bfloat16

VMEM_LIMIT_BYTES = 56 * 1024 * 1024

IN_PROJ_ROWS = 1024
MIXER_ROWS = 256
FFN_ROWS = 512
FFN_COLS = 512
HALO = 8


def _rms(x, g):
    ms = jnp.mean(x * x, axis=-1, keepdims=True)
    return x * lax.rsqrt(ms + EPS) * g


def _in_proj_kernel(x_ref, g_ref, w_ref, cos_ref, sin_ref, lng_ref, lnb_ref, o_ref, h_ref):
    j = pl.program_id(1)

    @pl.when(j == 0)
    def _():
        h_ref[...] = _rms(x_ref[...], g_ref[...]).astype(BF16)

    def proj():
        return jnp.dot(h_ref[...], w_ref[...], preferred_element_type=F32)

    def sigmoid(v):
        return 0.5 * jnp.tanh(0.5 * v) + 0.5

    def gelu(v):
        inner = v * (GELU_C + (GELU_C * 0.044715) * (v * v))
        return v * (0.5 * jnp.tanh(inner) + 0.5)

    def rotary_store(scale):
        acc = proj()
        c = cos_ref[...]
        s = sin_ref[...]
        for h in range(RET_HEADS):
            cols = slice(h * HEAD_DIM, (h + 1) * HEAD_DIM)
            xh = acc[:, cols]
            r = xh * c + pltpu.roll(xh, HEAD_DIM // 2, 1) * s
            if scale is not None:
                r = r * scale
            o_ref[:, cols] = r.astype(BF16)

    @pl.when(j == 0)
    def _():
        rotary_store(None)

    @pl.when(j == 1)
    def _():
        rotary_store(HEAD_DIM ** -0.5)

    @pl.when(j == 2)
    def _():
        o_ref[...] = proj().astype(BF16)

    @pl.when(j == 3)
    def _():
        acc = proj()
        o_ref[...] = (acc * sigmoid(acc)).astype(BF16)

    @pl.when(j == 4)
    def _():
        o_ref[...] = gelu(proj()).astype(BF16)

    @pl.when(j == 5)
    def _():
        v = gelu(proj())
        mu = jnp.mean(v, axis=-1, keepdims=True)
        vc = v - mu
        y = vc * lax.rsqrt(jnp.mean(vc * vc, axis=-1, keepdims=True) + EPS)
        o_ref[...] = (y * lng_ref[...] + lnb_ref[...]).astype(BF16)

    @pl.when(j >= 6)
    def _():
        o_ref[...] = sigmoid(proj()).astype(BF16)


def _in_proj(x2, pre_g, w_in, cosf, sinf, ln_g, ln_b, seq):
    t = x2.shape[0]
    tm = IN_PROJ_ROWS
    pos_blocks = seq // tm
    return pl.pallas_call(
        _in_proj_kernel,
        grid=(t // tm, IN_COLS // SEG),
        in_specs=[
            pl.BlockSpec((tm, D_MODEL), lambda i, j: (i, 0)),
            pl.BlockSpec((1, D_MODEL), lambda i, j: (0, 0)),
            pl.BlockSpec((D_MODEL, SEG), lambda i, j: (0, j)),
            pl.BlockSpec((tm, HEAD_DIM), lambda i, j: (i % pos_blocks, 0)),
            pl.BlockSpec((tm, HEAD_DIM), lambda i, j: (i % pos_blocks, 0)),
            pl.BlockSpec((1, SEG), lambda i, j: (0, 0)),
            pl.BlockSpec((1, SEG), lambda i, j: (0, 0)),
        ],
        out_specs=pl.BlockSpec((tm, SEG), lambda i, j: (i, j)),
        out_shape=jax.ShapeDtypeStruct((t, IN_COLS), BF16),
        scratch_shapes=[pltpu.VMEM((tm, D_MODEL), BF16)],
        compiler_params=pltpu.CompilerParams(
            dimension_semantics=("arbitrary", "arbitrary"),
            vmem_limit_bytes=VMEM_LIMIT_BYTES),
        name="in_proj",
    )(x2, pre_g, w_in, cosf, sinf, ln_g, ln_b)


def _mixer_kernel(cdec_ref, q_ref, k_ref, v_ref, rg_ref, su_ref, sv_ref, gr_ref, gs_ref, x_ref,
                  wret_ref, wsgu_ref, wout_ref, ws_ref, bs_ref, pmg_ref,
                  dmask_ref, xi_ref, zeta_ref,
                  o_ref, state_ref, ret_buf, sgu_buf):
    @pl.when(pl.program_id(1) == 0)
    def _():
        state_ref[...] = jnp.zeros_like(state_ref)

    row_id = lax.broadcasted_iota(jnp.int32, (CHUNK, CHUNK), 0)
    col_id = lax.broadcasted_iota(jnp.int32, (CHUNK, CHUNK), 1)
    causal = row_id >= col_id
    w_mix = [jnp.where(causal, ws_ref[g], 0.0).astype(BF16) for g in range(SGU_GROUPS)]

    for h in range(RET_HEADS):
        cols = slice(h * HEAD_DIM, (h + 1) * HEAD_DIM)
        qh = q_ref[:, cols]
        kh = k_ref[:, cols]
        vh = v_ref[:, cols]
        scores = lax.dot_general(qh, kh, (((1,), (1,)), ((), ())), preferred_element_type=F32)
        scores = scores * dmask_ref[h]
        intra = jnp.dot(scores.astype(BF16), vh, preferred_element_type=F32)
        state = state_ref[h]
        inter = jnp.dot(qh, state.astype(BF16), preferred_element_type=F32) * xi_ref[h]
        kz = (kh.astype(F32) * zeta_ref[h]).astype(BF16)
        kv = lax.dot_general(kz, vh, (((0,), (0,)), ((), ())), preferred_element_type=F32)
        state_ref[h] = state * cdec_ref[h] + kv
        o = intra + inter
        o = o * lax.rsqrt(jnp.mean(o * o, axis=-1, keepdims=True) + EPS)
        ret_buf[:, cols] = (rg_ref[:, cols].astype(F32) * o).astype(BF16)
    for c in range(MIXER_ROWS // CHUNK):
        rows = slice(c * CHUNK, (c + 1) * CHUNK)
        for g in range(SGU_GROUPS):
            cols = slice(g * HEAD_DIM, (g + 1) * HEAD_DIM)
            mixed = jnp.dot(w_mix[g], sv_ref[rows, cols], preferred_element_type=F32) + bs_ref[g]
            sgu_buf[rows, cols] = (su_ref[rows, cols].astype(F32) * mixed).astype(BF16)

    a = jnp.dot(ret_buf[...], wret_ref[...], preferred_element_type=F32)
    b = jnp.dot(sgu_buf[...], wsgu_ref[...], preferred_element_type=F32)
    merged = gr_ref[...].astype(F32) * a + gs_ref[...].astype(F32) * b
    z = jnp.dot(merged.astype(BF16), wout_ref[...], preferred_element_type=F32)
    o_ref[...] = x_ref[...] + _rms(z, pmg_ref[...])


def _mixer(proj, x2, w_ret, w_sgu, w_out, w_s, bs_b, post_g, dmask, xi_b, zeta_b, cdec, batch, seq):
    t = x2.shape[0]
    tq = MIXER_ROWS
    nq = seq // tq

    def tok(width, col):
        return pl.BlockSpec((tq, width), lambda b, n, *_: (b * nq + n, col))

    def resident(shape):
        zeros = (0,) * len(shape)
        return pl.BlockSpec(shape, lambda b, n, *_: zeros, pipeline_mode=pl.Buffered(1))

    grid_spec = pltpu.PrefetchScalarGridSpec(
        num_scalar_prefetch=1,
        grid=(batch, nq),
        in_specs=[
            tok(SEG, 0), tok(SEG, 1), tok(SEG, 2), tok(SEG, 3), tok(SEG, 4), tok(SEG, 5),
            tok(D_MODEL, 3), tok(D_MODEL, 4),
            tok(D_MODEL, 0),
            resident((SEG, D_MODEL)), resident((SEG, D_MODEL)), resident((D_MODEL, D_MODEL)),
            resident((SGU_GROUPS, CHUNK, CHUNK)), resident((SGU_GROUPS, CHUNK, CHUNK)),
            resident((1, D_MODEL)),
            resident((RET_HEADS, MIXER_ROWS, MIXER_ROWS)), resident((RET_HEADS, MIXER_ROWS, HEAD_DIM)),
            resident((RET_HEADS, MIXER_ROWS, HEAD_DIM)),
        ],
        out_specs=pl.BlockSpec((tq, D_MODEL), lambda b, n, *_: (b * nq + n, 0)),
        scratch_shapes=[
            pltpu.VMEM((RET_HEADS, HEAD_DIM, HEAD_DIM), F32),
            pltpu.VMEM((tq, SEG), BF16),
            pltpu.VMEM((tq, SEG), BF16),
        ],
    )
    return pl.pallas_call(
        _mixer_kernel,
        grid_spec=grid_spec,
        out_shape=jax.ShapeDtypeStruct((t, D_MODEL), F32),
        compiler_params=pltpu.CompilerParams(
            dimension_semantics=("arbitrary", "arbitrary"),
            vmem_limit_bytes=VMEM_LIMIT_BYTES),
        name="mixer",
    )(cdec, proj, proj, proj, proj, proj, proj, proj, proj, x2,
      w_ret, w_sgu, w_out, w_s, bs_b, post_g, dmask, xi_b, zeta_b)


def _ffn_kernel(x_ref, g_ref, wg_ref, wv_ref, cwg_ref, cwv_ref, cbg_ref, cbv_ref, wd_ref, pg_ref,
                o_ref, h_ref, acc_ref, tail_g_ref, tail_v_ref, *, tiles_per_seq):
    i = pl.program_id(0)
    j = pl.program_id(1)

    @pl.when(j == 0)
    def _():
        h_ref[...] = _rms(x_ref[...], g_ref[...]).astype(BF16)
        acc_ref[...] = jnp.zeros_like(acc_ref)

    h = h_ref[...]
    rows = h.shape[0]
    first_tile = (i % tiles_per_seq) == 0

    @pl.when(first_tile)
    def _():
        tail_g_ref[j] = jnp.zeros((HALO, FFN_COLS), F32)
        tail_v_ref[j] = jnp.zeros((HALO, FFN_COLS), F32)

    def conv(w_ref, cw, cb, tail_ref):
        u = jnp.dot(h, w_ref[...], preferred_element_type=F32)
        ucat = jnp.concatenate([tail_ref[j], u], axis=0)
        tail_ref[j] = u[rows - HALO:, :]
        return (ucat[HALO - 2:HALO - 2 + rows] * cw[0:1, :] + ucat[HALO - 1:HALO - 1 + rows] * cw[1:2, :]
                + u * cw[2:3, :] + cb)

    gate = conv(wg_ref, cwg_ref[...], cbg_ref[...], tail_g_ref)
    half_val = conv(wv_ref, 0.5 * cwv_ref[...], 0.5 * cbv_ref[...], tail_v_ref)
    inner = gate * (GELU_C + (GELU_C * 0.044715) * (gate * gate))
    act = ((gate * half_val) * (1.0 + jnp.tanh(inner))).astype(BF16)
    acc_ref[...] += jnp.dot(act, wd_ref[...], preferred_element_type=F32)

    @pl.when(j == pl.num_programs(1) - 1)
    def _():
        o_ref[...] = x_ref[...] + _rms(acc_ref[...], pg_ref[...])


def _ffn(x1, pre_g, w_up, conv_w, conv_b, w_down, post_g, seq):
    t = x1.shape[0]
    tm, tf = FFN_ROWS, FFN_COLS
    nf = D_FF // tf
    kernel = functools.partial(_ffn_kernel, tiles_per_seq=seq // tm)
    return pl.pallas_call(
        kernel,
        grid=(t // tm, nf),
        in_specs=[
            pl.BlockSpec((tm, D_MODEL), lambda i, j: (i, 0)),
            pl.BlockSpec((1, D_MODEL), lambda i, j: (0, 0)),
            pl.BlockSpec((D_MODEL, tf), lambda i, j: (0, j)),
            pl.BlockSpec((D_MODEL, tf), lambda i, j: (0, j + nf)),
            pl.BlockSpec((CONV_WIDTH, tf), lambda i, j: (0, j)),
            pl.BlockSpec((CONV_WIDTH, tf), lambda i, j: (0, j + nf)),
            pl.BlockSpec((1, tf), lambda i, j: (0, j)),
            pl.BlockSpec((1, tf), lambda i, j: (0, j + nf)),
            pl.BlockSpec((tf, D_MODEL), lambda i, j: (j, 0)),
            pl.BlockSpec((1, D_MODEL), lambda i, j: (0, 0)),
        ],
        out_specs=pl.BlockSpec((tm, D_MODEL), lambda i, j: (i, 0)),
        out_shape=jax.ShapeDtypeStruct((t, D_MODEL), F32),
        scratch_shapes=[
            pltpu.VMEM((tm, D_MODEL), BF16),
            pltpu.VMEM((tm, D_MODEL), F32),
            pltpu.VMEM((nf, HALO, tf), F32),
            pltpu.VMEM((nf, HALO, tf), F32),
        ],
        compiler_params=pltpu.CompilerParams(
            dimension_semantics=("arbitrary", "arbitrary"),
            vmem_limit_bytes=VMEM_LIMIT_BYTES),
        name="ffn",
    )(x1, pre_g, w_up, w_up, conv_w, conv_w, conv_b, conv_b, w_down, post_g)


def _rotary_tables(seq):
    inv_freq = 1.0 / (ROPE_BASE ** (jnp.arange(0, HEAD_DIM, 2, dtype=F32) / HEAD_DIM))
    ang = jnp.arange(seq, dtype=F32)[:, None] * inv_freq[None, :]
    cos, sin = jnp.cos(ang), jnp.sin(ang)
    return jnp.concatenate([cos, cos], axis=-1), jnp.concatenate([-sin, sin], axis=-1)


def _decay_tables():
    log_gamma = jnp.log(1.0 - 2.0 ** (-5.0 - jnp.arange(RET_HEADS, dtype=F32)))
    idx = jnp.arange(MIXER_ROWS, dtype=F32)
    rel = idx[:, None] - idx[None, :]
    causal = rel >= 0
    dmask = jnp.where(causal[None],
                      jnp.exp(log_gamma[:, None, None] * jnp.where(causal, rel, 0.0)[None]), 0.0)
    xi = jnp.exp(log_gamma[:, None] * (idx + 1.0)[None])
    zeta = jnp.exp(log_gamma[:, None] * (MIXER_ROWS - 1.0 - idx)[None])
    cdec = jnp.exp(log_gamma * MIXER_ROWS)
    bshape = (RET_HEADS, MIXER_ROWS, HEAD_DIM)
    return (dmask, jnp.broadcast_to(xi[:, :, None], bshape),
            jnp.broadcast_to(zeta[:, :, None], bshape), cdec)


def kernel(x, pre_mix_g, w_in, w_ret_proj, sgu_ln_g, sgu_ln_b, w_s, b_s, w_sgu_proj, w_out,
           post_mix_g, pre_ffn_g, w_up, conv_w, conv_b, w_down, post_ffn_g):
    batch, seq, _ = x.shape
    depth = w_in.shape[0]
    cosf, sinf = _rotary_tables(seq)
    dmask, xi_b, zeta_b, cdec = _decay_tables()
    x2 = x.reshape(batch * seq, D_MODEL)
    for l in range(depth):
        proj = _in_proj(x2, pre_mix_g[l][None], w_in[l].astype(BF16), cosf, sinf,
                        sgu_ln_g[l][None], sgu_ln_b[l][None], seq)
        bs_b = jnp.broadcast_to(b_s[l][:, :, None], (SGU_GROUPS, CHUNK, CHUNK))
        x2 = _mixer(proj, x2, w_ret_proj[l].astype(BF16), w_sgu_proj[l].astype(BF16),
                    w_out[l].astype(BF16), w_s[l], bs_b, post_mix_g[l][None],
                    dmask, xi_b, zeta_b, cdec, batch, seq)
        x2 = _ffn(x2, pre_ffn_g[l][None], w_up[l].astype(BF16), conv_w[l], conv_b[l][None],
                  w_down[l].astype(BF16), post_ffn_g[l][None], seq)
    return x2.reshape(batch, seq, D_MODEL)
```

```python
import functools

import jax
import jax.numpy as jnp
import numpy as np
from jax import lax
from jax.experimental import pallas as pl
from jax.experimental.pallas import tpu as pltpu

D_MODEL = 2048
RET_HEADS = 8
HEAD_DIM = 128
SGU_GROUPS = 8
SEG = 1024
CHUNK = 128
D_FF = 5632
CONV_WIDTH = 3
ROPE_BASE = 10000.0
EPS = 1e-6
GELU_C = float(np.sqrt(2.0 / np.pi))
IN_COLS = 6 * SEG + 2 * D_MODEL

F32 = jnp.float32
BF16 = jnp.bfloat16

VMEM_LIMIT_BYTES = 56 * 1024 * 1024

IN_PROJ_ROWS = 1024
MIXER_ROWS = 256
FFN_ROWS = 512
FFN_COLS = 512
CAST_SHAPES = ((SEG, D_MODEL), (SEG, D_MODEL), (D_MODEL, D_MODEL), (D_MODEL, 2 * D_FF), (D_FF, D_MODEL))
CAST_COL_BLOCKS = 8
HALO = 8


def _rms(x, g):
    ms = jnp.mean(x * x, axis=-1, keepdims=True)
    return x * lax.rsqrt(ms + EPS) * g


def _in_proj_kernel(x_ref, g_ref, w_ref, cos_ref, sin_ref, lng_ref, lnb_ref, *rest):
    n_cast = len(CAST_SHAPES)
    cast_in, o_ref, cast_out, h_ref = rest[:n_cast], rest[n_cast], rest[n_cast + 1:2 * n_cast + 1], rest[-1]
    j = pl.program_id(1)

    @pl.when(j < CAST_COL_BLOCKS)
    def _():
        for src_ref, dst_ref in zip(cast_in, cast_out):
            dst_ref[...] = src_ref[...].astype(BF16)

    @pl.when(j == 0)
    def _():
        h_ref[...] = _rms(x_ref[...], g_ref[...]).astype(BF16)

    def proj():
        return jnp.dot(h_ref[...], w_ref[...], preferred_element_type=F32)

    def sigmoid(v):
        return 0.5 * jnp.tanh(0.5 * v) + 0.5

    def gelu(v):
        inner = v * (GELU_C + (GELU_C * 0.044715) * (v * v))
        return v * (0.5 * jnp.tanh(inner) + 0.5)

    def rotary_store(scale):
        acc = proj()
        c = cos_ref[...]
        s = sin_ref[...]
        for h in range(RET_HEADS):
            cols = slice(h * HEAD_DIM, (h + 1) * HEAD_DIM)
            xh = acc[:, cols]
            r = xh * c + pltpu.roll(xh, HEAD_DIM // 2, 1) * s
            if scale is not None:
                r = r * scale
            o_ref[:, cols] = r.astype(BF16)

    @pl.when(j == 0)
    def _():
        rotary_store(None)

    @pl.when(j == 1)
    def _():
        rotary_store(HEAD_DIM ** -0.5)

    @pl.when(j == 2)
    def _():
        o_ref[...] = proj().astype(BF16)

    @pl.when(j == 3)
    def _():
        acc = proj()
        o_ref[...] = (acc * sigmoid(acc)).astype(BF16)

    @pl.when(j == 4)
    def _():
        o_ref[...] = gelu(proj()).astype(BF16)

    @pl.when(j == 5)
    def _():
        v = gelu(proj())
        mu = jnp.mean(v, axis=-1, keepdims=True)
        vc = v - mu
        y = vc * lax.rsqrt(jnp.mean(vc * vc, axis=-1, keepdims=True) + EPS)
        o_ref[...] = (y * lng_ref[...] + lnb_ref[...]).astype(BF16)

    @pl.when(j >= 6)
    def _():
        o_ref[...] = sigmoid(proj()).astype(BF16)


def _in_proj(x2, pre_g, w_in, cosf, sinf, ln_g, ln_b, cast_weights, seq):
    t = x2.shape[0]
    tm = IN_PROJ_ROWS
    pos_blocks = seq // tm
    row_blocks = t // tm
    assert tuple(w.shape for w in cast_weights) == CAST_SHAPES
    cast_specs = [
        pl.BlockSpec((r // row_blocks, c // CAST_COL_BLOCKS),
                     lambda i, j: (i, jnp.minimum(j, CAST_COL_BLOCKS - 1)))
        for r, c in CAST_SHAPES]
    outs = pl.pallas_call(
        _in_proj_kernel,
        grid=(t // tm, IN_COLS // SEG),
        in_specs=[
            pl.BlockSpec((tm, D_MODEL), lambda i, j: (i, 0)),
            pl.BlockSpec((1, D_MODEL), lambda i, j: (0, 0)),
            pl.BlockSpec((D_MODEL, SEG), lambda i, j: (0, j)),
            pl.BlockSpec((tm, HEAD_DIM), lambda i, j: (i % pos_blocks, 0)),
            pl.BlockSpec((tm, HEAD_DIM), lambda i, j: (i % pos_blocks, 0)),
            pl.BlockSpec((1, SEG), lambda i, j: (0, 0)),
            pl.BlockSpec((1, SEG), lambda i, j: (0, 0)),
        ] + cast_specs,
        out_specs=[pl.BlockSpec((tm, SEG), lambda i, j: (i, j))] + cast_specs,
        out_shape=[jax.ShapeDtypeStruct((t, IN_COLS), BF16)]
        + [jax.ShapeDtypeStruct(s, BF16) for s in CAST_SHAPES],
        scratch_shapes=[pltpu.VMEM((tm, D_MODEL), BF16)],
        compiler_params=pltpu.CompilerParams(
            dimension_semantics=("arbitrary", "arbitrary"),
            vmem_limit_bytes=VMEM_LIMIT_BYTES),
        name="in_proj",
    )(x2, pre_g, w_in, cosf, sinf, ln_g, ln_b, *cast_weights)
    return outs[0], outs[1:]


def _mixer_kernel(cdec_ref, q_ref, k_ref, v_ref, rg_ref, su_ref, sv_ref, gr_ref, gs_ref, x_ref,
                  wret_ref, wsgu_ref, wout_ref, ws_ref, bs_ref, pmg_ref,
                  dmask_ref, xi_ref, zeta_ref,
                  o_ref, state_ref, ret_buf, sgu_buf):
    @pl.when(pl.program_id(1) == 0)
    def _():
        state_ref[...] = jnp.zeros_like(state_ref)

    row_id = lax.broadcasted_iota(jnp.int32, (CHUNK, CHUNK), 0)
    col_id = lax.broadcasted_iota(jnp.int32, (CHUNK, CHUNK), 1)
    causal = row_id >= col_id
    w_mix = [jnp.where(causal, ws_ref[g], 0.0).astype(BF16) for g in range(SGU_GROUPS)]

    for h in range(RET_HEADS):
        cols = slice(h * HEAD_DIM, (h + 1) * HEAD_DIM)
        qh = q_ref[:, cols]
        kh = k_ref[:, cols]
        vh = v_ref[:, cols]
        scores = lax.dot_general(qh, kh, (((1,), (1,)), ((), ())), preferred_element_type=F32)
        scores = scores * dmask_ref[h]
        intra = jnp.dot(scores.astype(BF16), vh, preferred_element_type=F32)
        state = state_ref[h]
        inter = jnp.dot(qh, state.astype(BF16), preferred_element_type=F32) * xi_ref[h]
        kz = (kh.astype(F32) * zeta_ref[h]).astype(BF16)
        kv = lax.dot_general(kz, vh, (((0,), (0,)), ((), ())), preferred_element_type=F32)
        state_ref[h] = state * cdec_ref[h] + kv
        o = intra + inter
        o = o * lax.rsqrt(jnp.mean(o * o, axis=-1, keepdims=True) + EPS)
        ret_buf[:, cols] = (rg_ref[:, cols].astype(F32) * o).astype(BF16)
    for c in range(MIXER_ROWS // CHUNK):
        rows = slice(c * CHUNK, (c + 1) * CHUNK)
        for g in range(SGU_GROUPS):
            cols = slice(g * HEAD_DIM, (g + 1) * HEAD_DIM)
            mixed = jnp.dot(w_mix[g], sv_ref[rows, cols], preferred_element_type=F32) + bs_ref[g]
            sgu_buf[rows, cols] = (su_ref[rows, cols].astype(F32) * mixed).astype(BF16)

    a = jnp.dot(ret_buf[...], wret_ref[...], preferred_element_type=F32)
    b = jnp.dot(sgu_buf[...], wsgu_ref[...], preferred_element_type=F32)
    merged = gr_ref[...].astype(F32) * a + gs_ref[...].astype(F32) * b
    z = jnp.dot(merged.astype(BF16), wout_ref[...], preferred_element_type=F32)
    o_ref[...] = x_ref[...] + _rms(z, pmg_ref[...])


def _mixer(proj, x2, w_ret, w_sgu, w_out, w_s, bs_b, post_g, dmask, xi_b, zeta_b, cdec, batch, seq):
    t = x2.shape[0]
    tq = MIXER_ROWS
    nq = seq // tq

    def tok(width, col):
        return pl.BlockSpec((tq, width), lambda b, n, *_: (b * nq + n, col))

    def resident(shape):
        zeros = (0,) * len(shape)
        return pl.BlockSpec(shape, lambda b, n, *_: zeros, pipeline_mode=pl.Buffered(1))

    grid_spec = pltpu.PrefetchScalarGridSpec(
        num_scalar_prefetch=1,
        grid=(batch, nq),
        in_specs=[
            tok(SEG, 0), tok(SEG, 1), tok(SEG, 2), tok(SEG, 3), tok(SEG, 4), tok(SEG, 5),
            tok(D_MODEL, 3), tok(D_MODEL, 4),
            tok(D_MODEL, 0),
            resident((SEG, D_MODEL)), resident((SEG, D_MODEL)), resident((D_MODEL, D_MODEL)),
            resident((SGU_GROUPS, CHUNK, CHUNK)), resident((SGU_GROUPS, CHUNK, CHUNK)),
            resident((1, D_MODEL)),
            resident((RET_HEADS, MIXER_ROWS, MIXER_ROWS)), resident((RET_HEADS, MIXER_ROWS, HEAD_DIM)),
            resident((RET_HEADS, MIXER_ROWS, HEAD_DIM)),
        ],
        out_specs=pl.BlockSpec((tq, D_MODEL), lambda b, n, *_: (b * nq + n, 0)),
        scratch_shapes=[
            pltpu.VMEM((RET_HEADS, HEAD_DIM, HEAD_DIM), F32),
            pltpu.VMEM((tq, SEG), BF16),
            pltpu.VMEM((tq, SEG), BF16),
        ],
    )
    return pl.pallas_call(
        _mixer_kernel,
        grid_spec=grid_spec,
        out_shape=jax.ShapeDtypeStruct((t, D_MODEL), F32),
        compiler_params=pltpu.CompilerParams(
            dimension_semantics=("arbitrary", "arbitrary"),
            vmem_limit_bytes=VMEM_LIMIT_BYTES),
        name="mixer",
    )(cdec, proj, proj, proj, proj, proj, proj, proj, proj, x2,
      w_ret, w_sgu, w_out, w_s, bs_b, post_g, dmask, xi_b, zeta_b)


def _ffn_kernel(x_ref, g_ref, wg_ref, wv_ref, cwg_ref, cwv_ref, cbg_ref, cbv_ref, wd_ref, pg_ref,
                o_ref, h_ref, acc_ref, tail_g_ref, tail_v_ref, *, tiles_per_seq):
    i = pl.program_id(0)
    j = pl.program_id(1)

    @pl.when(j == 0)
    def _():
        h_ref[...] = _rms(x_ref[...], g_ref[...]).astype(BF16)
        acc_ref[...] = jnp.zeros_like(acc_ref)

    h = h_ref[...]
    rows = h.shape[0]
    first_tile = (i % tiles_per_seq) == 0

    @pl.when(first_tile)
    def _():
        tail_g_ref[j] = jnp.zeros((HALO, FFN_COLS), F32)
        tail_v_ref[j] = jnp.zeros((HALO, FFN_COLS), F32)

    def conv(w_ref, cw, cb, tail_ref):
        u = jnp.dot(h, w_ref[...], preferred_element_type=F32)
        ucat = jnp.concatenate([tail_ref[j], u], axis=0)
        tail_ref[j] = u[rows - HALO:, :]
        return (ucat[HALO - 2:HALO - 2 + rows] * cw[0:1, :] + ucat[HALO - 1:HALO - 1 + rows] * cw[1:2, :]
                + u * cw[2:3, :] + cb)

    gate = conv(wg_ref, cwg_ref[...], cbg_ref[...], tail_g_ref)
    half_val = conv(wv_ref, 0.5 * cwv_ref[...], 0.5 * cbv_ref[...], tail_v_ref)
    inner = gate * (GELU_C + (GELU_C * 0.044715) * (gate * gate))
    act = ((gate * half_val) * (1.0 + jnp.tanh(inner))).astype(BF16)
    acc_ref[...] += jnp.dot(act, wd_ref[...], preferred_element_type=F32)

    @pl.when(j == pl.num_programs(1) - 1)
    def _():
        o_ref[...] = x_ref[...] + _rms(acc_ref[...], pg_ref[...])


def _ffn(x1, pre_g, w_up, conv_w, conv_b, w_down, post_g, seq):
    t = x1.shape[0]
    tm, tf = FFN_ROWS, FFN_COLS
    nf = D_FF // tf
    kernel = functools.partial(_ffn_kernel, tiles_per_seq=seq // tm)
    return pl.pallas_call(
        kernel,
        grid=(t // tm, nf),
        in_specs=[
            pl.BlockSpec((tm, D_MODEL), lambda i, j: (i, 0)),
            pl.BlockSpec((1, D_MODEL), lambda i, j: (0, 0)),
            pl.BlockSpec((D_MODEL, tf), lambda i, j: (0, j)),
            pl.BlockSpec((D_MODEL, tf), lambda i, j: (0, j + nf)),
            pl.BlockSpec((CONV_WIDTH, tf), lambda i, j: (0, j)),
            pl.BlockSpec((CONV_WIDTH, tf), lambda i, j: (0, j + nf)),
            pl.BlockSpec((1, tf), lambda i, j: (0, j)),
            pl.BlockSpec((1, tf), lambda i, j: (0, j + nf)),
            pl.BlockSpec((tf, D_MODEL), lambda i, j: (j, 0)),
            pl.BlockSpec((1, D_MODEL), lambda i, j: (0, 0)),
        ],
        out_specs=pl.BlockSpec((tm, D_MODEL), lambda i, j: (i, 0)),
        out_shape=jax.ShapeDtypeStruct((t, D_MODEL), F32),
        scratch_shapes=[
            pltpu.VMEM((tm, D_MODEL), BF16),
            pltpu.VMEM((tm, D_MODEL), F32),
            pltpu.VMEM((nf, HALO, tf), F32),
            pltpu.VMEM((nf, HALO, tf), F32),
        ],
        compiler_params=pltpu.CompilerParams(
            dimension_semantics=("arbitrary", "arbitrary"),
            vmem_limit_bytes=VMEM_LIMIT_BYTES),
        name="ffn",
    )(x1, pre_g, w_up, w_up, conv_w, conv_w, conv_b, conv_b, w_down, post_g)


def _rotary_tables(seq):
    inv_freq = 1.0 / (ROPE_BASE ** (jnp.arange(0, HEAD_DIM, 2, dtype=F32) / HEAD_DIM))
    ang = jnp.arange(seq, dtype=F32)[:, None] * inv_freq[None, :]
    cos, sin = jnp.cos(ang), jnp.sin(ang)
    return jnp.concatenate([cos, cos], axis=-1), jnp.concatenate([-sin, sin], axis=-1)


def _decay_tables():
    log_gamma = jnp.log(1.0 - 2.0 ** (-5.0 - jnp.arange(RET_HEADS, dtype=F32)))
    idx = jnp.arange(MIXER_ROWS, dtype=F32)
    rel = idx[:, None] - idx[None, :]
    causal = rel >= 0
    dmask = jnp.where(causal[None],
                      jnp.exp(log_gamma[:, None, None] * jnp.where(causal, rel, 0.0)[None]), 0.0)
    xi = jnp.exp(log_gamma[:, None] * (idx + 1.0)[None])
    zeta = jnp.exp(log_gamma[:, None] * (MIXER_ROWS - 1.0 - idx)[None])
    cdec = jnp.exp(log_gamma * MIXER_ROWS)
    bshape = (RET_HEADS, MIXER_ROWS, HEAD_DIM)
    return (dmask, jnp.broadcast_to(xi[:, :, None], bshape),
            jnp.broadcast_to(zeta[:, :, None], bshape), cdec)


def kernel(x, pre_mix_g, w_in, w_ret_proj, sgu_ln_g, sgu_ln_b, w_s, b_s, w_sgu_proj, w_out,
           post_mix_g, pre_ffn_g, w_up, conv_w, conv_b, w_down, post_ffn_g):
    batch, seq, _ = x.shape
    depth = w_in.shape[0]
    cosf, sinf = _rotary_tables(seq)
    dmask, xi_b, zeta_b, cdec = _decay_tables()
    x2 = x.reshape(batch * seq, D_MODEL)
    for l in range(depth):
        proj, (w_ret16, w_sgu16, w_out16, w_up16, w_down16) = _in_proj(
            x2, pre_mix_g[l][None], w_in[l].astype(BF16), cosf, sinf, sgu_ln_g[l][None], sgu_ln_b[l][None],
            (w_ret_proj[l], w_sgu_proj[l], w_out[l], w_up[l], w_down[l]), seq)
        bs_b = jnp.broadcast_to(b_s[l][:, :, None], (SGU_GROUPS, CHUNK, CHUNK))
        x2 = _mixer(proj, x2, w_ret16, w_sgu16, w_out16, w_s[l], bs_b, post_mix_g[l][None],
                    dmask, xi_b, zeta_b, cdec, batch, seq)
        x2 = _ffn(x2, pre_ffn_g[l][None], w_up16, conv_w[l], conv_b[l][None], w_down16,
                  post_ffn_g[l][None], seq)
    return x2.reshape(batch, seq, D_MODEL)
```

```python
import functools

import jax
import jax.numpy as jnp
import numpy as np
from jax import lax
from jax.experimental import pallas as pl
from jax.experimental.pallas import tpu as pltpu

D_MODEL = 2048
RET_HEADS = 8
HEAD_DIM = 128
SGU_GROUPS = 8
SEG = 1024
CHUNK = 128
D_FF = 5632
CONV_WIDTH = 3
ROPE_BASE = 10000.0
EPS = 1e-6
GELU_C = float(np.sqrt(2.0 / np.pi))
IN_COLS = 6 * SEG + 2 * D_MODEL

F32 = jnp.float32
BF16 = jnp.bfloat16

VMEM_LIMIT_BYTES = 56 * 1024 * 1024

IN_PROJ_ROWS = 1024
MIXER_ROWS = 256
FFN_ROWS = 512
FFN_COLS = 512
CAST_SHAPES = ((SEG, D_MODEL), (SEG, D_MODEL), (D_MODEL, D_MODEL), (D_MODEL, 2 * D_FF), (D_FF, D_MODEL))
CAST_COL_BLOCKS = 8
HALO = 8


def _rms(x, g):
    ms = jnp.mean(x * x, axis=-1, keepdims=True)
    return x * lax.rsqrt(ms + EPS) * g


def _in_proj_kernel(x_ref, g_ref, w_ref, cos_ref, sin_ref, lng_ref, lnb_ref, *rest):
    n_cast = len(CAST_SHAPES)
    cast_in, o_ref, cast_out, h_ref = rest[:n_cast], rest[n_cast], rest[n_cast + 1:2 * n_cast + 1], rest[-1]
    j = pl.program_id(1)

    @pl.when(j < CAST_COL_BLOCKS)
    def _():
        for src_ref, dst_ref in zip(cast_in, cast_out):
            dst_ref[...] = src_ref[...].astype(BF16)

    @pl.when(j == 0)
    def _():
        h_ref[...] = _rms(x_ref[...], g_ref[...]).astype(BF16)

    def proj():
        return jnp.dot(h_ref[...], w_ref[...], preferred_element_type=F32)

    def sigmoid(v):
        return 0.5 * jnp.tanh(0.5 * v) + 0.5

    def gelu(v):
        inner = v * (GELU_C + (GELU_C * 0.044715) * (v * v))
        return v * (0.5 * jnp.tanh(inner) + 0.5)

    def rotary_store(scale):
        acc = proj()
        c = cos_ref[...]
        s = sin_ref[...]
        for h in range(RET_HEADS):
            cols = slice(h * HEAD_DIM, (h + 1) * HEAD_DIM)
            xh = acc[:, cols]
            r = xh * c + pltpu.roll(xh, HEAD_DIM // 2, 1) * s
            if scale is not None:
                r = r * scale
            o_ref[:, cols] = r.astype(BF16)

    @pl.when(j == 0)
    def _():
        rotary_store(None)

    @pl.when(j == 1)
    def _():
        rotary_store(HEAD_DIM ** -0.5)

    @pl.when(j == 2)
    def _():
        o_ref[...] = proj().astype(BF16)

    @pl.when(j == 3)
    def _():
        acc = proj()
        o_ref[...] = (acc * sigmoid(acc)).astype(BF16)

    @pl.when(j == 4)
    def _():
        o_ref[...] = gelu(proj()).astype(BF16)

    @pl.when(j == 5)
    def _():
        v = gelu(proj())
        mu = jnp.mean(v, axis=-1, keepdims=True)
        vc = v - mu
        y = vc * lax.rsqrt(jnp.mean(vc * vc, axis=-1, keepdims=True) + EPS)
        o_ref[...] = (y * lng_ref[...] + lnb_ref[...]).astype(BF16)

    @pl.when(j >= 6)
    def _():
        o_ref[...] = sigmoid(proj()).astype(BF16)


def _in_proj(x2, pre_g, w_in, cosf, sinf, ln_g, ln_b, cast_weights, seq):
    t = x2.shape[0]
    tm = IN_PROJ_ROWS
    pos_blocks = seq // tm
    row_blocks = t // tm
    assert tuple(w.shape for w in cast_weights) == CAST_SHAPES
    cast_specs = [
        pl.BlockSpec((r // row_blocks, c // CAST_COL_BLOCKS),
                     lambda i, j: (i, jnp.minimum(j, CAST_COL_BLOCKS - 1)))
        for r, c in CAST_SHAPES]
    outs = pl.pallas_call(
        _in_proj_kernel,
        grid=(t // tm, IN_COLS // SEG),
        in_specs=[
            pl.BlockSpec((tm, D_MODEL), lambda i, j: (i, 0)),
            pl.BlockSpec((1, D_MODEL), lambda i, j: (0, 0)),
            pl.BlockSpec((D_MODEL, SEG), lambda i, j: (0, j)),
            pl.BlockSpec((tm, HEAD_DIM), lambda i, j: (i % pos_blocks, 0)),
            pl.BlockSpec((tm, HEAD_DIM), lambda i, j: (i % pos_blocks, 0)),
            pl.BlockSpec((1, SEG), lambda i, j: (0, 0)),
            pl.BlockSpec((1, SEG), lambda i, j: (0, 0)),
        ] + cast_specs,
        out_specs=[pl.BlockSpec((tm, SEG), lambda i, j: (i, j))] + cast_specs,
        out_shape=[jax.ShapeDtypeStruct((t, IN_COLS), BF16)]
        + [jax.ShapeDtypeStruct(s, BF16) for s in CAST_SHAPES],
        scratch_shapes=[pltpu.VMEM((tm, D_MODEL), BF16)],
        compiler_params=pltpu.CompilerParams(
            dimension_semantics=("arbitrary", "arbitrary"),
            vmem_limit_bytes=VMEM_LIMIT_BYTES),
        name="in_proj",
    )(x2, pre_g, w_in, cosf, sinf, ln_g, ln_b, *cast_weights)
    return outs[0], outs[1:]


def _mixer_kernel(cdec_ref, q_ref, k_ref, v_ref, rg_ref, su_ref, sv_ref, gr_ref, gs_ref, x_ref,
                  wret_ref, wsgu_ref, wout_ref, ws_ref, bs_ref, pmg_ref,
                  dmask_ref, xi_ref, zeta_ref,
                  o_ref, state_ref, ret_buf, sgu_buf):
    @pl.when(pl.program_id(1) == 0)
    def _():
        state_ref[...] = jnp.zeros_like(state_ref)

    row_id = lax.broadcasted_iota(jnp.int32, (CHUNK, CHUNK), 0)
    col_id = lax.broadcasted_iota(jnp.int32, (CHUNK, CHUNK), 1)
    causal = row_id >= col_id
    w_mix = [jnp.where(causal, ws_ref[g], 0.0).astype(BF16) for g in range(SGU_GROUPS)]

    for h in range(RET_HEADS):
        cols = slice(h * HEAD_DIM, (h + 1) * HEAD_DIM)
        qh = q_ref[:, cols]
        kh = k_ref[:, cols]
        vh = v_ref[:, cols]
        scores = lax.dot_general(qh, kh, (((1,), (1,)), ((), ())), preferred_element_type=F32)
        scores = scores * dmask_ref[h]
        intra = jnp.dot(scores.astype(BF16), vh, preferred_element_type=F32)
        state = state_ref[h]
        inter = jnp.dot(qh, state.astype(BF16), preferred_element_type=F32) * xi_ref[h]
        kz = (kh.astype(F32) * zeta_ref[h]).astype(BF16)
        kv = lax.dot_general(kz, vh, (((0,), (0,)), ((), ())), preferred_element_type=F32)
        state_ref[h] = state * cdec_ref[h] + kv
        o = intra + inter
        o = o * lax.rsqrt(jnp.mean(o * o, axis=-1, keepdims=True) + EPS)
        ret_buf[:, cols] = (rg_ref[:, cols].astype(F32) * o).astype(BF16)
    for c in range(MIXER_ROWS // CHUNK):
        rows = slice(c * CHUNK, (c + 1) * CHUNK)
        for g in range(SGU_GROUPS):
            cols = slice(g * HEAD_DIM, (g + 1) * HEAD_DIM)
            mixed = jnp.dot(w_mix[g], sv_ref[rows, cols], preferred_element_type=F32) + bs_ref[g]
            sgu_buf[rows, cols] = (su_ref[rows, cols].astype(F32) * mixed).astype(BF16)

    a = jnp.dot(ret_buf[...], wret_ref[...], preferred_element_type=F32)
    b = jnp.dot(sgu_buf[...], wsgu_ref[...], preferred_element_type=F32)
    merged = gr_ref[...].astype(F32) * a + gs_ref[...].astype(F32) * b
    z = jnp.dot(merged.astype(BF16), wout_ref[...], preferred_element_type=F32)
    o_ref[...] = x_ref[...] + _rms(z, pmg_ref[...])


def _mixer(proj, x2, w_ret, w_sgu, w_out, w_s, bs_b, post_g, dmask, xi_b, zeta_b, cdec, batch, seq):
    t = x2.shape[0]
    tq = MIXER_ROWS
    nq = seq // tq

    def tok(width, col):
        return pl.BlockSpec((tq, width), lambda b, n, *_: (b * nq + n, col))

    def resident(shape):
        zeros = (0,) * len(shape)
        return pl.BlockSpec(shape, lambda b, n, *_: zeros, pipeline_mode=pl.Buffered(1))

    grid_spec = pltpu.PrefetchScalarGridSpec(
        num_scalar_prefetch=1,
        grid=(batch, nq),
        in_specs=[
            tok(SEG, 0), tok(SEG, 1), tok(SEG, 2), tok(SEG, 3), tok(SEG, 4), tok(SEG, 5),
            tok(D_MODEL, 3), tok(D_MODEL, 4),
            tok(D_MODEL, 0),
            resident((SEG, D_MODEL)), resident((SEG, D_MODEL)), resident((D_MODEL, D_MODEL)),
            resident((SGU_GROUPS, CHUNK, CHUNK)), resident((SGU_GROUPS, CHUNK, CHUNK)),
            resident((1, D_MODEL)),
            resident((RET_HEADS, MIXER_ROWS, MIXER_ROWS)), resident((RET_HEADS, MIXER_ROWS, HEAD_DIM)),
            resident((RET_HEADS, MIXER_ROWS, HEAD_DIM)),
        ],
        out_specs=pl.BlockSpec((tq, D_MODEL), lambda b, n, *_: (b * nq + n, 0)),
        scratch_shapes=[
            pltpu.VMEM((RET_HEADS, HEAD_DIM, HEAD_DIM), F32),
            pltpu.VMEM((tq, SEG), BF16),
            pltpu.VMEM((tq, SEG), BF16),
        ],
    )
    return pl.pallas_call(
        _mixer_kernel,
        grid_spec=grid_spec,
        out_shape=jax.ShapeDtypeStruct((t, D_MODEL), F32),
        compiler_params=pltpu.CompilerParams(
            dimension_semantics=("arbitrary", "arbitrary"),
            vmem_limit_bytes=VMEM_LIMIT_BYTES),
        name="mixer",
    )(cdec, proj, proj, proj, proj, proj, proj, proj, proj, x2,
      w_ret, w_sgu, w_out, w_s, bs_b, post_g, dmask, xi_b, zeta_b)


def _ffn_kernel(x_ref, g_ref, wg_ref, wv_ref, cwg_ref, cwv_ref, cbg_ref, cbv_ref, wd_ref, pg_ref,
                o_ref, h_ref, acc_ref, tail_g_ref, tail_v_ref, *, tiles_per_seq):
    i = pl.program_id(0)
    j = pl.program_id(1)

    @pl.when(j == 0)
    def _():
        h_ref[...] = _rms(x_ref[...], g_ref[...]).astype(BF16)
        acc_ref[...] = jnp.zeros_like(acc_ref)

    rows = h_ref.shape[0]
    first_tile = (i % tiles_per_seq) == 0

    @pl.when(first_tile)
    def _():
        tail_g_ref[j] = jnp.zeros((HALO, FFN_COLS), F32)
        tail_v_ref[j] = jnp.zeros((HALO, FFN_COLS), F32)

    def conv(w_ref, cw, cb, tail_ref):
        u = jnp.dot(h_ref[...], w_ref[...], preferred_element_type=F32)
        ucat = jnp.concatenate([tail_ref[j], u], axis=0)
        tail_ref[j] = u[rows - HALO:, :]
        return (ucat[HALO - 2:HALO - 2 + rows] * cw[0:1, :] + ucat[HALO - 1:HALO - 1 + rows] * cw[1:2, :]
                + u * cw[2:3, :] + cb)

    gate = conv(wg_ref, cwg_ref[...], cbg_ref[...], tail_g_ref)
    half_val = conv(wv_ref, 0.5 * cwv_ref[...], 0.5 * cbv_ref[...], tail_v_ref)
    inner = gate * (GELU_C + (GELU_C * 0.044715) * (gate * gate))
    act = ((gate * half_val) * (1.0 + jnp.tanh(inner))).astype(BF16)
    acc_ref[...] += jnp.dot(act, wd_ref[...], preferred_element_type=F32)

    @pl.when(j == pl.num_programs(1) - 1)
    def _():
        o_ref[...] = x_ref[...] + _rms(acc_ref[...], pg_ref[...])


def _ffn(x1, pre_g, w_up, conv_w, conv_b, w_down, post_g, seq):
    t = x1.shape[0]
    tm, tf = FFN_ROWS, FFN_COLS
    nf = D_FF // tf
    kernel = functools.partial(_ffn_kernel, tiles_per_seq=seq // tm)
    return pl.pallas_call(
        kernel,
        grid=(t // tm, nf),
        in_specs=[
            pl.BlockSpec((tm, D_MODEL), lambda i, j: (i, 0)),
            pl.BlockSpec((1, D_MODEL), lambda i, j: (0, 0)),
            pl.BlockSpec((D_MODEL, tf), lambda i, j: (0, j)),
            pl.BlockSpec((D_MODEL, tf), lambda i, j: (0, j + nf)),
            pl.BlockSpec((CONV_WIDTH, tf), lambda i, j: (0, j)),
            pl.BlockSpec((CONV_WIDTH, tf), lambda i, j: (0, j + nf)),
            pl.BlockSpec((1, tf), lambda i, j: (0, j)),
            pl.BlockSpec((1, tf), lambda i, j: (0, j + nf)),
            pl.BlockSpec((tf, D_MODEL), lambda i, j: (j, 0)),
            pl.BlockSpec((1, D_MODEL), lambda i, j: (0, 0)),
        ],
        out_specs=pl.BlockSpec((tm, D_MODEL), lambda i, j: (i, 0)),
        out_shape=jax.ShapeDtypeStruct((t, D_MODEL), F32),
        scratch_shapes=[
            pltpu.VMEM((tm, D_MODEL), BF16),
            pltpu.VMEM((tm, D_MODEL), F32),
            pltpu.VMEM((nf, HALO, tf), F32),
            pltpu.VMEM((nf, HALO, tf), F32),
        ],
        compiler_params=pltpu.CompilerParams(
            dimension_semantics=("arbitrary", "arbitrary"),
            vmem_limit_bytes=VMEM_LIMIT_BYTES),
        name="ffn",
    )(x1, pre_g, w_up, w_up, conv_w, conv_w, conv_b, conv_b, w_down, post_g)


def _rotary_tables(seq):
    inv_freq = 1.0 / (ROPE_BASE ** (jnp.arange(0, HEAD_DIM, 2, dtype=F32) / HEAD_DIM))
    ang = jnp.arange(seq, dtype=F32)[:, None] * inv_freq[None, :]
    cos, sin = jnp.cos(ang), jnp.sin(ang)
    return jnp.concatenate([cos, cos], axis=-1), jnp.concatenate([-sin, sin], axis=-1)


def _decay_tables():
    log_gamma = jnp.log(1.0 - 2.0 ** (-5.0 - jnp.arange(RET_HEADS, dtype=F32)))
    idx = jnp.arange(MIXER_ROWS, dtype=F32)
    rel = idx[:, None] - idx[None, :]
    causal = rel >= 0
    dmask = jnp.where(causal[None],
                      jnp.exp(log_gamma[:, None, None] * jnp.where(causal, rel, 0.0)[None]), 0.0)
    xi = jnp.exp(log_gamma[:, None] * (idx + 1.0)[None])
    zeta = jnp.exp(log_gamma[:, None] * (MIXER_ROWS - 1.0 - idx)[None])
    cdec = jnp.exp(log_gamma * MIXER_ROWS)
    bshape = (RET_HEADS, MIXER_ROWS, HEAD_DIM)
    return (dmask, jnp.broadcast_to(xi[:, :, None], bshape),
            jnp.broadcast_to(zeta[:, :, None], bshape), cdec)


def kernel(x, pre_mix_g, w_in, w_ret_proj, sgu_ln_g, sgu_ln_b, w_s, b_s, w_sgu_proj, w_out,
           post_mix_g, pre_ffn_g, w_up, conv_w, conv_b, w_down, post_ffn_g):
    batch, seq, _ = x.shape
    depth = w_in.shape[0]
    cosf, sinf = _rotary_tables(seq)
    dmask, xi_b, zeta_b, cdec = _decay_tables()
    x2 = x.reshape(batch * seq, D_MODEL)
    for l in range(depth):
        proj, (w_ret16, w_sgu16, w_out16, w_up16, w_down16) = _in_proj(
            x2, pre_mix_g[l][None], w_in[l].astype(BF16), cosf, sinf, sgu_ln_g[l][None], sgu_ln_b[l][None],
            (w_ret_proj[l], w_sgu_proj[l], w_out[l], w_up[l], w_down[l]), seq)
        bs_b = jnp.broadcast_to(b_s[l][:, :, None], (SGU_GROUPS, CHUNK, CHUNK))
        x2 = _mixer(proj, x2, w_ret16, w_sgu16, w_out16, w_s[l], bs_b, post_mix_g[l][None],
                    dmask, xi_b, zeta_b, cdec, batch, seq)
        x2 = _ffn(x2, pre_ffn_g[l][None], w_up16, conv_w[l], conv_b[l][None], w_down16,
                  post_ffn_g[l][None], seq)
    return x2.reshape(batch, seq, D_MODEL)
```

```python
import functools

import jax
import jax.numpy as jnp
import numpy as np
from jax import lax
from jax.experimental import pallas as pl
from jax.experimental.pallas import tpu as pltpu

D_MODEL = 2048
RET_HEADS = 8
HEAD_DIM = 128
SGU_GROUPS = 8
SEG = 1024
CHUNK = 128
D_FF = 5632
CONV_WIDTH = 3
ROPE_BASE = 10000.0
EPS = 1e-6
GELU_C = float(np.sqrt(2.0 / np.pi))
IN_COLS = 6 * SEG + 2 * D_MODEL

F32 = jnp.float32
BF16 = jnp.bfloat16

VMEM_LIMIT_BYTES = 56 * 1024 * 1024

IN_PROJ_ROWS = 1024
MIXER_ROWS = 256
FFN_ROWS = 512
FFN_COLS = 512
CAST_SHAPES = ((SEG, D_MODEL), (SEG, D_MODEL), (D_MODEL, D_MODEL), (D_MODEL, 2 * D_FF), (D_FF, D_MODEL))
CAST_COL_BLOCKS = 8
HALO = 8


def _rms(x, g):
    ms = jnp.mean(x * x, axis=-1, keepdims=True)
    return x * lax.rsqrt(ms + EPS) * g


def _in_proj_kernel(x_ref, g_ref, w_ref, cos_ref, sin_ref, lng_ref, lnb_ref, *rest):
    n_cast = len(CAST_SHAPES)
    cast_in, o_ref, cast_out, h_ref = rest[:n_cast], rest[n_cast], rest[n_cast + 1:2 * n_cast + 1], rest[-1]
    j = pl.program_id(1)

    @pl.when(j < CAST_COL_BLOCKS)
    def _():
        for src_ref, dst_ref in zip(cast_in, cast_out):
            dst_ref[...] = src_ref[...].astype(BF16)

    @pl.when(j == 0)
    def _():
        h_ref[...] = _rms(x_ref[...], g_ref[...]).astype(BF16)

    def proj():
        return jnp.dot(h_ref[...], w_ref[...].astype(BF16), preferred_element_type=F32)

    def sigmoid(v):
        return 0.5 * jnp.tanh(0.5 * v) + 0.5

    def gelu(v):
        inner = v * (GELU_C + (GELU_C * 0.044715) * (v * v))
        return v * (0.5 * jnp.tanh(inner) + 0.5)

    def rotary_store(scale):
        acc = proj()
        c = cos_ref[...]
        s = sin_ref[...]
        for h in range(RET_HEADS):
            cols = slice(h * HEAD_DIM, (h + 1) * HEAD_DIM)
            xh = acc[:, cols]
            r = xh * c + pltpu.roll(xh, HEAD_DIM // 2, 1) * s
            if scale is not None:
                r = r * scale
            o_ref[:, cols] = r.astype(BF16)

    @pl.when(j == 0)
    def _():
        rotary_store(None)

    @pl.when(j == 1)
    def _():
        rotary_store(HEAD_DIM ** -0.5)

    @pl.when(j == 2)
    def _():
        o_ref[...] = proj().astype(BF16)

    @pl.when(j == 3)
    def _():
        acc = proj()
        o_ref[...] = (acc * sigmoid(acc)).astype(BF16)

    @pl.when(j == 4)
    def _():
        o_ref[...] = gelu(proj()).astype(BF16)

    @pl.when(j == 5)
    def _():
        v = gelu(proj())
        mu = jnp.mean(v, axis=-1, keepdims=True)
        vc = v - mu
        y = vc * lax.rsqrt(jnp.mean(vc * vc, axis=-1, keepdims=True) + EPS)
        o_ref[...] = (y * lng_ref[...] + lnb_ref[...]).astype(BF16)

    @pl.when(j >= 6)
    def _():
        o_ref[...] = sigmoid(proj()).astype(BF16)


def _in_proj(x2, pre_g, w_in, cosf, sinf, ln_g, ln_b, cast_weights, seq):
    t = x2.shape[0]
    tm = IN_PROJ_ROWS
    pos_blocks = seq // tm
    row_blocks = t // tm
    assert tuple(w.shape for w in cast_weights) == CAST_SHAPES
    cast_specs = [
        pl.BlockSpec((r // row_blocks, c // CAST_COL_BLOCKS),
                     lambda i, j: (i, jnp.minimum(j, CAST_COL_BLOCKS - 1)))
        for r, c in CAST_SHAPES]
    outs = pl.pallas_call(
        _in_proj_kernel,
        grid=(t // tm, IN_COLS // SEG),
        in_specs=[
            pl.BlockSpec((tm, D_MODEL), lambda i, j: (i, 0)),
            pl.BlockSpec((1, D_MODEL), lambda i, j: (0, 0)),
            pl.BlockSpec((D_MODEL, SEG), lambda i, j: (0, j)),
            pl.BlockSpec((tm, HEAD_DIM), lambda i, j: (i % pos_blocks, 0)),
            pl.BlockSpec((tm, HEAD_DIM), lambda i, j: (i % pos_blocks, 0)),
            pl.BlockSpec((1, SEG), lambda i, j: (0, 0)),
            pl.BlockSpec((1, SEG), lambda i, j: (0, 0)),
        ] + cast_specs,
        out_specs=[pl.BlockSpec((tm, SEG), lambda i, j: (i, j))] + cast_specs,
        out_shape=[jax.ShapeDtypeStruct((t, IN_COLS), BF16)]
        + [jax.ShapeDtypeStruct(s, BF16) for s in CAST_SHAPES],
        scratch_shapes=[pltpu.VMEM((tm, D_MODEL), BF16)],
        compiler_params=pltpu.CompilerParams(
            dimension_semantics=("arbitrary", "arbitrary"),
            vmem_limit_bytes=VMEM_LIMIT_BYTES),
        name="in_proj",
    )(x2, pre_g, w_in, cosf, sinf, ln_g, ln_b, *cast_weights)
    return outs[0], outs[1:]


def _mixer_kernel(cdec_ref, seg_ref, gr_ref, gs_ref, x_ref,
                  wret_ref, wsgu_ref, wout_ref, ws_ref, bs_ref, pmg_ref,
                  dmask_ref, xi_ref, zeta_ref,
                  o_ref, state_ref, ret_buf, sgu_buf):
    @pl.when(pl.program_id(1) == 0)
    def _():
        state_ref[...] = jnp.zeros_like(state_ref)

    q_ref, k_ref, v_ref, rg_ref, su_ref, sv_ref = (seg_ref.at[:, n * SEG:(n + 1) * SEG] for n in range(6))

    row_id = lax.broadcasted_iota(jnp.int32, (CHUNK, CHUNK), 0)
    col_id = lax.broadcasted_iota(jnp.int32, (CHUNK, CHUNK), 1)
    causal = row_id >= col_id
    w_mix = [jnp.where(causal, ws_ref[g], 0.0).astype(BF16) for g in range(SGU_GROUPS)]

    for h in range(RET_HEADS):
        cols = slice(h * HEAD_DIM, (h + 1) * HEAD_DIM)
        qh = q_ref[:, cols]
        kh = k_ref[:, cols]
        vh = v_ref[:, cols]
        scores = lax.dot_general(qh, kh, (((1,), (1,)), ((), ())), preferred_element_type=F32)
        scores = scores * dmask_ref[h]
        intra = jnp.dot(scores.astype(BF16), vh, preferred_element_type=F32)
        state = state_ref[h]
        inter = jnp.dot(qh, state.astype(BF16), preferred_element_type=F32) * xi_ref[h]
        kz = (kh.astype(F32) * zeta_ref[h]).astype(BF16)
        kv = lax.dot_general(kz, vh, (((0,), (0,)), ((), ())), preferred_element_type=F32)
        state_ref[h] = state * cdec_ref[h] + kv
        o = intra + inter
        o = o * lax.rsqrt(jnp.mean(o * o, axis=-1, keepdims=True) + EPS)
        ret_buf[:, cols] = (rg_ref[:, cols].astype(F32) * o).astype(BF16)
    for c in range(MIXER_ROWS // CHUNK):
        rows = slice(c * CHUNK, (c + 1) * CHUNK)
        for g in range(SGU_GROUPS):
            cols = slice(g * HEAD_DIM, (g + 1) * HEAD_DIM)
            mixed = jnp.dot(w_mix[g], sv_ref[rows, cols], preferred_element_type=F32) + bs_ref[g]
            sgu_buf[rows, cols] = (su_ref[rows, cols].astype(F32) * mixed).astype(BF16)

    a = jnp.dot(ret_buf[...], wret_ref[...], preferred_element_type=F32)
    b = jnp.dot(sgu_buf[...], wsgu_ref[...], preferred_element_type=F32)
    merged = gr_ref[...].astype(F32) * a + gs_ref[...].astype(F32) * b
    z = jnp.dot(merged.astype(BF16), wout_ref[...], preferred_element_type=F32)
    o_ref[...] = x_ref[...] + _rms(z, pmg_ref[...])


def _mixer(proj, x2, w_ret, w_sgu, w_out, w_s, bs_b, post_g, dmask, xi_b, zeta_b, cdec, batch, seq):
    t = x2.shape[0]
    tq = MIXER_ROWS
    nq = seq // tq

    def tok(width, col):
        return pl.BlockSpec((tq, width), lambda b, n, *_: (b * nq + n, col))

    def resident(shape):
        zeros = (0,) * len(shape)
        return pl.BlockSpec(shape, lambda b, n, *_: zeros, pipeline_mode=pl.Buffered(1))

    grid_spec = pltpu.PrefetchScalarGridSpec(
        num_scalar_prefetch=1,
        grid=(batch, nq),
        in_specs=[
            tok(6 * SEG, 0), tok(D_MODEL, 3), tok(D_MODEL, 4),
            tok(D_MODEL, 0),
            resident((SEG, D_MODEL)), resident((SEG, D_MODEL)), resident((D_MODEL, D_MODEL)),
            resident((SGU_GROUPS, CHUNK, CHUNK)), resident((SGU_GROUPS, CHUNK, CHUNK)),
            resident((1, D_MODEL)),
            resident((RET_HEADS, MIXER_ROWS, MIXER_ROWS)), resident((RET_HEADS, MIXER_ROWS, HEAD_DIM)),
            resident((RET_HEADS, MIXER_ROWS, HEAD_DIM)),
        ],
        out_specs=pl.BlockSpec((tq, D_MODEL), lambda b, n, *_: (b * nq + n, 0)),
        scratch_shapes=[
            pltpu.VMEM((RET_HEADS, HEAD_DIM, HEAD_DIM), F32),
            pltpu.VMEM((tq, SEG), BF16),
            pltpu.VMEM((tq, SEG), BF16),
        ],
    )
    return pl.pallas_call(
        _mixer_kernel,
        grid_spec=grid_spec,
        out_shape=jax.ShapeDtypeStruct((t, D_MODEL), F32),
        compiler_params=pltpu.CompilerParams(
            dimension_semantics=("arbitrary", "arbitrary"),
            vmem_limit_bytes=VMEM_LIMIT_BYTES),
        name="mixer",
    )(cdec, proj, proj, proj, x2,
      w_ret, w_sgu, w_out, w_s, bs_b, post_g, dmask, xi_b, zeta_b)


def _ffn_kernel(x_ref, g_ref, wg_ref, wv_ref, cw_ref, cb_ref, wd_ref, pg_ref,
                o_ref, h_ref, acc_ref, tail_g_ref, tail_v_ref, *, tiles_per_seq):
    i = pl.program_id(0)
    j = pl.program_id(1)

    @pl.when(j == 0)
    def _():
        h_ref[...] = _rms(x_ref[...], g_ref[...]).astype(BF16)
        acc_ref[...] = jnp.zeros_like(acc_ref)

    rows = h_ref.shape[0]
    first_tile = (i % tiles_per_seq) == 0

    @pl.when(first_tile)
    def _():
        tail_g_ref[j] = jnp.zeros((HALO, FFN_COLS), F32)
        tail_v_ref[j] = jnp.zeros((HALO, FFN_COLS), F32)

    def conv(w_ref, cw, cb, tail_ref):
        u = jnp.dot(h_ref[...], w_ref[...], preferred_element_type=F32)
        ucat = jnp.concatenate([tail_ref[j], u], axis=0)
        tail_ref[j] = u[rows - HALO:, :]
        return (ucat[HALO - 2:HALO - 2 + rows] * cw[0:1, :] + ucat[HALO - 1:HALO - 1 + rows] * cw[1:2, :]
                + u * cw[2:3, :] + cb)

    gcols = pl.ds(pl.multiple_of(j * FFN_COLS, FFN_COLS), FFN_COLS)
    vcols = pl.ds(pl.multiple_of(D_FF + j * FFN_COLS, FFN_COLS), FFN_COLS)
    gate = conv(wg_ref, cw_ref[:, gcols], cb_ref[:, gcols], tail_g_ref)
    half_val = conv(wv_ref, 0.5 * cw_ref[:, vcols], 0.5 * cb_ref[:, vcols], tail_v_ref)
    inner = gate * (GELU_C + (GELU_C * 0.044715) * (gate * gate))
    act = ((gate * half_val) * (1.0 + jnp.tanh(inner))).astype(BF16)
    acc_ref[...] += jnp.dot(act, wd_ref[...], preferred_element_type=F32)

    @pl.when(j == pl.num_programs(1) - 1)
    def _():
        o_ref[...] = x_ref[...] + _rms(acc_ref[...], pg_ref[...])


def _ffn(x1, pre_g, w_up, conv_w, conv_b, w_down, post_g, seq):
    t = x1.shape[0]
    tm, tf = FFN_ROWS, FFN_COLS
    nf = D_FF // tf
    kernel = functools.partial(_ffn_kernel, tiles_per_seq=seq // tm)
    return pl.pallas_call(
        kernel,
        grid=(t // tm, nf),
        in_specs=[
            pl.BlockSpec((tm, D_MODEL), lambda i, j: (i, 0)),
            pl.BlockSpec((1, D_MODEL), lambda i, j: (0, 0)),
            pl.BlockSpec((D_MODEL, tf), lambda i, j: (0, j)),
            pl.BlockSpec((D_MODEL, tf), lambda i, j: (0, j + nf)),
            pl.BlockSpec((CONV_WIDTH, 2 * D_FF), lambda i, j: (0, 0)),
            pl.BlockSpec((1, 2 * D_FF), lambda i, j: (0, 0)),
            pl.BlockSpec((tf, D_MODEL), lambda i, j: (j, 0)),
            pl.BlockSpec((1, D_MODEL), lambda i, j: (0, 0)),
        ],
        out_specs=pl.BlockSpec((tm, D_MODEL), lambda i, j: (i, 0)),
        out_shape=jax.ShapeDtypeStruct((t, D_MODEL), F32),
        scratch_shapes=[
            pltpu.VMEM((tm, D_MODEL), BF16),
            pltpu.VMEM((tm, D_MODEL), F32),
            pltpu.VMEM((nf, HALO, tf), F32),
            pltpu.VMEM((nf, HALO, tf), F32),
        ],
        compiler_params=pltpu.CompilerParams(
            dimension_semantics=("arbitrary", "arbitrary"),
            vmem_limit_bytes=VMEM_LIMIT_BYTES),
        name="ffn",
    )(x1, pre_g, w_up, w_up, conv_w, conv_b, w_down, post_g)


def _rotary_tables(seq):
    inv_freq = 1.0 / (ROPE_BASE ** (jnp.arange(0, HEAD_DIM, 2, dtype=F32) / HEAD_DIM))
    ang = jnp.arange(seq, dtype=F32)[:, None] * inv_freq[None, :]
    cos, sin = jnp.cos(ang), jnp.sin(ang)
    return jnp.concatenate([cos, cos], axis=-1), jnp.concatenate([-sin, sin], axis=-1)


def _decay_tables():
    log_gamma = jnp.log(1.0 - 2.0 ** (-5.0 - jnp.arange(RET_HEADS, dtype=F32)))
    idx = jnp.arange(MIXER_ROWS, dtype=F32)
    rel = idx[:, None] - idx[None, :]
    causal = rel >= 0
    dmask = jnp.where(causal[None],
                      jnp.exp(log_gamma[:, None, None] * jnp.where(causal, rel, 0.0)[None]), 0.0)
    xi = jnp.exp(log_gamma[:, None] * (idx + 1.0)[None])
    zeta = jnp.exp(log_gamma[:, None] * (MIXER_ROWS - 1.0 - idx)[None])
    cdec = jnp.exp(log_gamma * MIXER_ROWS)
    bshape = (RET_HEADS, MIXER_ROWS, HEAD_DIM)
    return (dmask, jnp.broadcast_to(xi[:, :, None], bshape),
            jnp.broadcast_to(zeta[:, :, None], bshape), cdec)


def kernel(x, pre_mix_g, w_in, w_ret_proj, sgu_ln_g, sgu_ln_b, w_s, b_s, w_sgu_proj, w_out,
           post_mix_g, pre_ffn_g, w_up, conv_w, conv_b, w_down, post_ffn_g):
    batch, seq, _ = x.shape
    depth = w_in.shape[0]
    cosf, sinf = _rotary_tables(seq)
    dmask, xi_b, zeta_b, cdec = _decay_tables()
    x2 = x.reshape(batch * seq, D_MODEL)
    for l in range(depth):
        proj, (w_ret16, w_sgu16, w_out16, w_up16, w_down16) = _in_proj(
            x2, pre_mix_g[l][None], w_in[l], cosf, sinf, sgu_ln_g[l][None], sgu_ln_b[l][None],
            (w_ret_proj[l], w_sgu_proj[l], w_out[l], w_up[l], w_down[l]), seq)
        bs_b = jnp.broadcast_to(b_s[l][:, :, None], (SGU_GROUPS, CHUNK, CHUNK))
        x2 = _mixer(proj, x2, w_ret16, w_sgu16, w_out16, w_s[l], bs_b, post_mix_g[l][None],
                    dmask, xi_b, zeta_b, cdec, batch, seq)
        x2 = _ffn(x2, pre_ffn_g[l][None], w_up16, conv_w[l], conv_b[l][None], w_down16,
                  post_ffn_g[l][None], seq)
    return x2.reshape(batch, seq, D_MODEL)
```
